```python
import math
import jax, jax.numpy as jnp
from jax import lax
import numpy as np

D_MODEL = 1024
BATCH = 16
SEQ = 2048
DEPTH = 2

MEM_LEN = 256
A_HEADS = 4
A_DH = 64
A_DV = 2 * A_DH
B_GROUPS = 4
B_DG = 64
B_WIDTH = B_GROUPS * B_DG
B_CHUNK = 128
C_HEADS = 4
C_DH = 64
C_BLOCK = 256
C_TOPK = 3
C_QCHUNK = 64
A_Q_COLS = A_HEADS * 2 * A_DH
A_K_COLS = A_HEADS * 2 * A_DH
A_V_COLS = A_HEADS * A_DV
B_COLS = 2 * B_WIDTH
C_COLS = C_HEADS * C_DH
MIX_WIDTH = A_HEADS * A_DV + B_WIDTH + C_HEADS * C_DH
IN_COLS = A_Q_COLS + A_K_COLS + A_V_COLS + B_COLS + 3 * C_COLS
XA_HEADS = 4
XA_DH = D_MODEL // XA_HEADS
D_FF = 2816
ROPE_THETA = 10000.0
ATTN_QBLOCK = 128
LN_EPS = 1e-5
DEEPNORM_ALPHA = (2.0 * DEPTH) ** 0.25
DEEPNORM_BETA = (8.0 * DEPTH) ** -0.25

kernel_name = "hymba_style_diff_sgu_moba_macaron_deepnorm"


def layer_norm(x, g, b):
    xf = x.astype(jnp.float32)
    mu = jnp.mean(xf, axis=-1, keepdims=True)
    var = jnp.mean(jnp.square(xf - mu), axis=-1, keepdims=True)
    y = (xf - mu) * lax.rsqrt(var + LN_EPS)
    return (y * g.astype(jnp.float32) + b.astype(jnp.float32)).astype(x.dtype)


def rms_norm(x, g):
    xf = x.astype(jnp.float32)
    y = xf * lax.rsqrt(jnp.mean(jnp.square(xf), axis=-1, keepdims=True) + LN_EPS)
    return (y * g.astype(jnp.float32)).astype(x.dtype)


def swiglu(x, w_gate, w_up, w_down):
    return (jax.nn.silu(x @ w_gate) * (x @ w_up)) @ w_down


def rope_tables(positions, dim):
    inv_freq = 1.0 / (ROPE_THETA ** (jnp.arange(0, dim, 2, dtype=jnp.float32) / dim))
    ang = positions.astype(jnp.float32)[..., None] * inv_freq
    return jnp.cos(ang)[:, :, None, :], jnp.sin(ang)[:, :, None, :]


def apply_rope(x, cos, sin):
    xf = x.astype(jnp.float32)
    x1, x2 = jnp.split(xf, 2, axis=-1)
    return jnp.concatenate([x1 * cos - x2 * sin, x2 * cos + x1 * sin], axis=-1).astype(x.dtype)


def diff_attention(q1, q2, k1, k2, v, lam):
    Bsz, S, H, d = q1.shape
    nqb = S // ATTN_QBLOCK
    scale = d ** -0.5
    qs = jnp.stack([q1, q2], axis=0).reshape(2, Bsz, nqb, ATTN_QBLOCK, H, d)
    qs = qs.transpose(2, 0, 1, 3, 4, 5)
    ks = jnp.stack([k1, k2], axis=0)
    kpos = jnp.arange(S)

    def block(args):
        i, qb = args
        qpos = i * ATTN_QBLOCK + jnp.arange(ATTN_QBLOCK)
        mask = kpos[None, :] <= qpos[:, None]
        s = jnp.einsum('mbqhd,mbkhd->mbhqk', qb, ks).astype(jnp.float32) * scale
        p = jax.nn.softmax(jnp.where(mask, s, -jnp.inf), axis=-1)
        w = p[0] - lam * p[1]
        return jnp.einsum('bhqk,bkhe->bqhe', w.astype(v.dtype), v)

    o = lax.map(block, (jnp.arange(nqb), qs))
    return o.transpose(1, 0, 2, 3, 4).reshape(Bsz, S, H, v.shape[-1])


def spatial_gating(u, v, ln_g, ln_b, w_s, b_s):
    Bsz, S, _ = v.shape
    nc = S // B_CHUNK
    v = layer_norm(v, ln_g, ln_b)
    vc = v.reshape(Bsz, nc, B_CHUNK, B_GROUPS, B_DG)
    w = jnp.tril(w_s)
    mix = jnp.einsum('gts,bnsgc->bntgc', w, vc) + b_s.T[None, None, :, :, None]
    return u * mix.reshape(Bsz, S, B_WIDTH)


def moba_attention(q, k, v):
    Bsz, S, H, d = q.shape
    S_pad = -(-S // C_BLOCK) * C_BLOCK
    pad = S_pad - S
    padw = ((0, 0), (0, pad), (0, 0), (0, 0))
    q, k, v = jnp.pad(q, padw), jnp.pad(k, padw), jnp.pad(v, padw)
    nB = S_pad // C_BLOCK
    topk = min(C_TOPK, nB)
    scale = d ** -0.5

    kb = k.reshape(Bsz, nB, C_BLOCK, H, d)
    vb = v.reshape(Bsz, nB, C_BLOCK, H, d)
    kbar = jnp.mean(kb.astype(jnp.float32), axis=2)
    gate = jnp.einsum('bshd,bnhd->bshn', q.astype(jnp.float32), kbar)
    qblk = jnp.arange(S_pad) // C_BLOCK
    past = jnp.arange(nB)[None, :] < qblk[:, None]
    gate = jnp.where(past[None, :, None, :], gate, -jnp.inf)
    _, sel = lax.top_k(gate, topk)
    valid = sel < qblk[None, :, None, None]

    kbt = kb.transpose(0, 3, 1, 2, 4)
    vbt = vb.transpose(0, 3, 1, 2, 4)
    b_i = jnp.arange(Bsz)[:, None, None, None]
    h_i = jnp.arange(H)[None, None, :, None]
    nqc = S_pad // C_QCHUNK

    def chunk(c):
        start = c * C_QCHUNK
        qc = lax.dynamic_slice_in_dim(q, start, C_QCHUNK, axis=1)
        selc = lax.dynamic_slice_in_dim(sel, start, C_QCHUNK, axis=1)
        validc = lax.dynamic_slice_in_dim(valid, start, C_QCHUNK, axis=1)
        blk_start = (start // C_BLOCK) * C_BLOCK
        k_own = lax.dynamic_slice_in_dim(k, blk_start, C_BLOCK, axis=1)
        v_own = lax.dynamic_slice_in_dim(v, blk_start, C_BLOCK, axis=1)
        k_sel = kbt[b_i, h_i, selc]
        v_sel = vbt[b_i, h_i, selc]
        s_sel = jnp.einsum('bqhd,bqhjtd->bhqjt', qc, k_sel).astype(jnp.float32) * scale
        s_sel = jnp.where(validc.transpose(0, 2, 1, 3)[..., None], s_sel, -jnp.inf)
        s_own = jnp.einsum('bqhd,bthd->bhqt', qc, k_own).astype(jnp.float32) * scale
        qpos = start + jnp.arange(C_QCHUNK)
        kpos = blk_start + jnp.arange(C_BLOCK)
        s_own = jnp.where(kpos[None, :] <= qpos[:, None], s_own, -jnp.inf)
        s = jnp.concatenate([s_sel.reshape(Bsz, H, C_QCHUNK, topk * C_BLOCK), s_own], axis=-1)
        p = jax.nn.softmax(s, axis=-1).astype(v.dtype)
        p_sel = p[..., :topk * C_BLOCK].reshape(Bsz, H, C_QCHUNK, topk, C_BLOCK)
        p_own = p[..., topk * C_BLOCK:]
        return (jnp.einsum('bhqjt,bqhjtd->bqhd', p_sel, v_sel)
                + jnp.einsum('bhqt,bthd->bqhd', p_own, v_own))

    o = lax.map(chunk, jnp.arange(nqc))
    o = o.transpose(1, 0, 2, 3, 4).reshape(Bsz, S_pad, H, d)
    return o[:, :S]


def hybrid_mixer(x, w_in, lq1, lk1, lq2, lk2, subln_g, sgu_ln_g, sgu_ln_b, sgu_w, sgu_b,
                 w_out, cos_a, sin_a, cos_c, sin_c, lam_init):
    Bsz, S, _ = x.shape
    h = x @ w_in
    cuts = np.cumsum([A_Q_COLS, A_K_COLS, A_V_COLS, B_WIDTH, B_WIDTH, C_COLS, C_COLS]).tolist()
    h_qa, h_ka, h_va, h_u, h_v, h_qc, h_kc, h_vc = jnp.split(h, cuts, axis=-1)

    qa = h_qa.reshape(Bsz, S, A_HEADS, 2, A_DH)
    ka = h_ka.reshape(Bsz, S, A_HEADS, 2, A_DH)
    q1 = apply_rope(qa[..., 0, :], cos_a, sin_a)
    q2 = apply_rope(qa[..., 1, :], cos_a, sin_a)
    k1 = apply_rope(ka[..., 0, :], cos_a, sin_a)
    k2 = apply_rope(ka[..., 1, :], cos_a, sin_a)
    va = h_va.reshape(Bsz, S, A_HEADS, A_DV)
    f32 = jnp.float32
    lam = (jnp.exp(jnp.sum(lq1.astype(f32) * lk1.astype(f32)))
           - jnp.exp(jnp.sum(lq2.astype(f32) * lk2.astype(f32))) + lam_init)
    oa = diff_attention(q1, q2, k1, k2, va, lam)
    oa = (rms_norm(oa, subln_g) * (1.0 - lam_init)).reshape(Bsz, S, A_HEADS * A_DV)

    ob = spatial_gating(jax.nn.gelu(h_u, approximate=False), jax.nn.gelu(h_v, approximate=False),
                        sgu_ln_g, sgu_ln_b, sgu_w, sgu_b)

    qc = apply_rope(h_qc.reshape(Bsz, S, C_HEADS, C_DH), cos_c, sin_c)
    kc = apply_rope(h_kc.reshape(Bsz, S, C_HEADS, C_DH), cos_c, sin_c)
    vc = h_vc.reshape(Bsz, S, C_HEADS, C_DH)
    oc = moba_attention(qc, kc, vc).reshape(Bsz, S, C_HEADS * C_DH)

    return jnp.concatenate([oa, ob, oc], axis=-1) @ w_out


def memory_cross_attention(x, mem, wq, wk, wv, wo):
    Bsz, S, _ = x.shape
    M = mem.shape[1]
    q = (x @ wq).reshape(Bsz, S, XA_HEADS, XA_DH)
    k = (mem @ wk).reshape(Bsz, M, XA_HEADS, XA_DH)
    v = (mem @ wv).reshape(Bsz, M, XA_HEADS, XA_DH)
    s = jnp.einsum('bshd,bmhd->bhsm', q, k).astype(jnp.float32) * (XA_DH ** -0.5)
    p = jax.nn.softmax(s, axis=-1).astype(v.dtype)
    o = jnp.einsum('bhsm,bmhd->bshd', p, v).reshape(Bsz, S, XA_HEADS * XA_DH)
    return o @ wo


def setup_inputs(seed: int = 0) -> dict:
    key = jax.random.key(seed)
    ks = iter(jax.random.split(key, 40))
    L, D, F = DEPTH, D_MODEL, D_FF

    def nrm(shape, scale):
        return jax.random.normal(next(ks), shape, jnp.float32) * scale

    def gain(shape):
        return 1.0 + nrm(shape, 0.02)

    x = nrm((BATCH, SEQ, D), 1.0)
    mem = nrm((BATCH, MEM_LEN, D), 1.0)
    offsets = jax.random.randint(next(ks), (BATCH, 1), 0, 4096, dtype=jnp.int32)
    positions = (offsets + jnp.arange(SEQ, dtype=jnp.int32)[None, :]).astype(jnp.int32)
    return {
        "x": x, "mem": mem, "positions": positions,
        "ffn1_w_gate": nrm((L, D, F), D ** -0.5),
        "ffn1_w_up": nrm((L, D, F), D ** -0.5),
        "ffn1_w_down": nrm((L, F, D), F ** -0.5 * DEEPNORM_BETA),
        "ln1_g": gain((L, D)), "ln1_b": nrm((L, D), 0.02),
        "mix_w_in": nrm((L, D, IN_COLS), D ** -0.5),
        "diff_lq1": nrm((L, A_DH), 0.1), "diff_lk1": nrm((L, A_DH), 0.1),
        "diff_lq2": nrm((L, A_DH), 0.1), "diff_lk2": nrm((L, A_DH), 0.1),
        "diff_subln_g": gain((L, A_DV)),
        "sgu_ln_g": gain((L, B_WIDTH)), "sgu_ln_b": nrm((L, B_WIDTH), 0.02),
        "sgu_w": nrm((L, B_GROUPS, B_CHUNK, B_CHUNK), B_CHUNK ** -0.5),
        "sgu_b": gain((L, B_GROUPS, B_CHUNK)),
        "mix_w_out": nrm((L, MIX_WIDTH, D), MIX_WIDTH ** -0.5 * DEEPNORM_BETA),
        "ln2_g": gain((L, D)), "ln2_b": nrm((L, D), 0.02),
        "xa_wq": nrm((L, D, XA_HEADS * XA_DH), D ** -0.5),
        "xa_wk": nrm((L, D, XA_HEADS * XA_DH), D ** -0.5),
        "xa_wv": nrm((L, D, XA_HEADS * XA_DH), D ** -0.5),
        "xa_wo": nrm((L, XA_HEADS * XA_DH, D), (XA_HEADS * XA_DH) ** -0.5 * DEEPNORM_BETA),
        "ln3_g": gain((L, D)), "ln3_b": nrm((L, D), 0.02),
        "ffn2_w_gate": nrm((L, D, F), D ** -0.5),
        "ffn2_w_up": nrm((L, D, F), D ** -0.5),
        "ffn2_w_down": nrm((L, F, D), F ** -0.5 * DEEPNORM_BETA),
        "ln4_g": gain((L, D)), "ln4_b": nrm((L, D), 0.02),
    }


def reference(x, mem, positions, ffn1_w_gate, ffn1_w_up, ffn1_w_down, ln1_g, ln1_b,
              mix_w_in, diff_lq1, diff_lk1, diff_lq2, diff_lk2, diff_subln_g,
              sgu_ln_g, sgu_ln_b, sgu_w, sgu_b, mix_w_out, ln2_g, ln2_b,
              xa_wq, xa_wk, xa_wv, xa_wo, ln3_g, ln3_b,
              ffn2_w_gate, ffn2_w_up, ffn2_w_down, ln4_g, ln4_b):
    cos_a, sin_a = rope_tables(positions, A_DH)
    cos_c, sin_c = rope_tables(positions, C_DH)
    for l in range(DEPTH):
        lam_init = 0.8 - 0.6 * math.exp(-0.3 * l)
        x = layer_norm(DEEPNORM_ALPHA * x + 0.5 * swiglu(x, ffn1_w_gate[l], ffn1_w_up[l], ffn1_w_down[l]),
                       ln1_g[l], ln1_b[l])
        mix = hybrid_mixer(x, mix_w_in[l], diff_lq1[l], diff_lk1[l], diff_lq2[l], diff_lk2[l],
                           diff_subln_g[l], sgu_ln_g[l], sgu_ln_b[l], sgu_w[l], sgu_b[l],
                           mix_w_out[l], cos_a, sin_a, cos_c, sin_c, lam_init)
        x = layer_norm(DEEPNORM_ALPHA * x + mix, ln2_g[l], ln2_b[l])
        xa = memory_cross_attention(x, mem, xa_wq[l], xa_wk[l], xa_wv[l], xa_wo[l])
        x = layer_norm(DEEPNORM_ALPHA * x + xa, ln3_g[l], ln3_b[l])
        x = layer_norm(DEEPNORM_ALPHA * x + 0.5 * swiglu(x, ffn2_w_gate[l], ffn2_w_up[l], ffn2_w_down[l]),
                       ln4_g[l], ln4_b[l])
    return x
```

```python
import functools
import math

import jax
import jax.numpy as jnp
from jax import lax
from jax.experimental import pallas as pl
from jax.experimental.pallas import tpu as pltpu

D_MODEL = 1024
DEPTH = 2
MEM_LEN = 256
A_HEADS = 4
A_DH = 64
A_DV = 2 * A_DH
B_GROUPS = 4
B_DG = 64
B_WIDTH = B_GROUPS * B_DG
B_CHUNK = 128
C_HEADS = 4
C_DH = 64
C_BLOCK = 256
C_TOPK = 3
A_COLS = 3 * A_HEADS * A_DV
C_COLS = 3 * C_HEADS * C_DH
IN_COLS = A_COLS + 2 * B_WIDTH + C_COLS
XA_HEADS = 4
XA_DH = D_MODEL // XA_HEADS
D_FF = 2816
ROPE_THETA = 10000.0
LN_EPS = 1e-5
DEEPNORM_ALPHA = (2.0 * DEPTH) ** 0.25

LANES = 128
MXU_WIDTH = 256
VMEM_LIMIT_BYTES = 56 * 1024 * 1024

ROW_TILE = 512
FF_CHUNK = MXU_WIDTH
ATTN_TILE = 512

F32 = jnp.float32
BF16 = jnp.bfloat16
NEG_INF = float("-inf")


def _params(*sem):
    return pltpu.CompilerParams(dimension_semantics=sem, vmem_limit_bytes=VMEM_LIMIT_BYTES)


def _resident(shape):
    zeros = (0,) * len(shape)
    return pl.BlockSpec(shape, lambda *_: zeros, pipeline_mode=pl.Buffered(1))


def _layer_norm(y, g, b):
    mu = jnp.mean(y, axis=-1, keepdims=True)
    d = y - mu
    var = jnp.mean(d * d, axis=-1, keepdims=True)
    return d * lax.rsqrt(var + LN_EPS) * g + b


def _dot(a, b):
    return jnp.dot(a, b, preferred_element_type=F32)


def _dot_nt(a, b):
    return lax.dot_general(a, b, (((1,), (1,)), ((), ())), preferred_element_type=F32)


def _rope_kernel(pos_ref, invf_ref, cos_ref, sin_ref):
    ang = pos_ref[...] * invf_ref[...]
    lane = lax.broadcasted_iota(jnp.int32, (1, LANES), 1)
    sign = jnp.where(lane % A_DH < A_DH // 2, -1.0, 1.0).astype(F32)
    cos_ref[...] = jnp.cos(ang)
    sin_ref[...] = jnp.sin(ang) * sign


def _rope_tables(positions):
    t = positions.size
    half = A_DH // 2
    inv_freq = 1.0 / (ROPE_THETA ** (jnp.arange(0, A_DH, 2, dtype=F32) / A_DH))
    invf = jnp.tile(inv_freq, LANES // half).reshape(1, LANES)
    pos = positions.astype(F32).reshape(t, 1)
    tm = 2048
    out = jax.ShapeDtypeStruct((t, LANES), F32)
    return pl.pallas_call(
        _rope_kernel,
        grid=(t // tm,),
        in_specs=[pl.BlockSpec((tm, 1), lambda i: (i, 0)),
                  pl.BlockSpec((1, LANES), lambda i: (0, 0))],
        out_specs=[pl.BlockSpec((tm, LANES), lambda i: (i, 0))] * 2,
        out_shape=[out, out],
        compiler_params=_params("parallel"),
        name="rope_tables",
    )(pos, invf)


def _ffn_ln_kernel(x_ref, wg_ref, wu_ref, wd_ref, g_ref, b_ref, o_ref, acc_ref):
    x = x_ref[...]
    xb = x.astype(BF16)
    for c in range(D_FF // FF_CHUNK):
        cols = slice(c * FF_CHUNK, (c + 1) * FF_CHUNK)
        gate = _dot(xb, wg_ref[:, cols])
        up = _dot(xb, wu_ref[:, cols])
        act = (gate * jax.nn.sigmoid(gate) * up).astype(BF16)
        part = _dot(act, wd_ref[cols, :])
        if c == 0:
            acc_ref[...] = part
        else:
            acc_ref[...] += part
    y = DEEPNORM_ALPHA * x + 0.5 * acc_ref[...]
    o_ref[...] = _layer_norm(y, g_ref[...], b_ref[...])


def _ffn_ln(x, wg, wu, wd, g, b):
    t = x.shape[0]
    row = pl.BlockSpec((ROW_TILE, D_MODEL), lambda i: (i, 0))
    return pl.pallas_call(
        _ffn_ln_kernel,
        grid=(t // ROW_TILE,),
        in_specs=[row, _resident((D_MODEL, D_FF)), _resident((D_MODEL, D_FF)),
                  _resident((D_FF, D_MODEL)), _resident((1, D_MODEL)), _resident((1, D_MODEL))],
        out_specs=row,
        out_shape=jax.ShapeDtypeStruct((t, D_MODEL), F32),
        scratch_shapes=[pltpu.VMEM((ROW_TILE, D_MODEL), F32)],
        compiler_params=_params("parallel"),
        name="ffn_ln",
    )(x, wg, wu, wd, g, b)


def _gelu(y):
    return 0.5 * y * (1.0 + lax.erf(y * (2.0 ** -0.5)))


def _inproj_kernel(x_ref, w_ref, cos_ref, sin_ref, lng_ref, lnb_ref, sw_ref, sb_ref,
                   a_ref, ob_ref, c_ref):
    xb = x_ref[...].astype(BF16)
    cos = cos_ref[...]
    sin = sin_ref[...]
    lane = lax.broadcasted_iota(jnp.int32, (1, LANES), 1)
    first_half = lane % A_DH < A_DH // 2
    low_lanes = lane < B_DG

    def rope(y, scale):
        swapped = jnp.where(first_half, pltpu.roll(y, LANES - A_DH // 2, 1),
                            pltpu.roll(y, A_DH // 2, 1))
        return ((y * cos + swapped * sin) * scale).astype(BF16)

    def project(col):
        return _dot(xb, w_ref[:, col:col + MXU_WIDTH])

    def halves(y):
        return y[:, :LANES], y[:, LANES:]

    q_scale = A_DH ** -0.5
    for j in range(2 * A_HEADS * A_DV // MXU_WIDTH):
        col = j * MXU_WIDTH
        scale = q_scale if col < A_HEADS * A_DV else 1.0
        lo, hi = halves(project(col))
        a_ref[:, col:col + LANES] = rope(lo, scale)
        a_ref[:, col + LANES:col + MXU_WIDTH] = rope(hi, scale)
    for j in range(A_HEADS * A_DV // MXU_WIDTH):
        col = 2 * A_HEADS * A_DV + j * MXU_WIDTH
        a_ref[:, col:col + MXU_WIDTH] = project(col).astype(BF16)

    u = _gelu(project(A_COLS))
    v = _gelu(project(A_COLS + B_WIDTH))
    vn = _layer_norm(v, lng_ref[...], lnb_ref[...]).astype(BF16)
    rows = lax.broadcasted_iota(jnp.int32, (B_CHUNK, B_CHUNK), 0)
    cols = lax.broadcasted_iota(jnp.int32, (B_CHUNK, B_CHUNK), 1)
    w_tril = [jnp.where(rows >= cols, sw_ref[g], 0.0).astype(BF16) for g in range(B_GROUPS)]
    zero = jnp.zeros((), BF16)
    for ci in range(ROW_TILE // B_CHUNK):
        rsl = slice(ci * B_CHUNK, (ci + 1) * B_CHUNK)
        for hb in range(B_WIDTH // LANES):
            csl = slice(hb * LANES, (hb + 1) * LANES)
            vblk = vn[rsl, csl]
            mix = (_dot(w_tril[2 * hb], jnp.where(low_lanes, vblk, zero))
                   + _dot(w_tril[2 * hb + 1], jnp.where(low_lanes, zero, vblk))
                   + sb_ref[:, csl])
            ob_ref[rsl, csl] = (u[rsl, csl] * mix).astype(BF16)

    c0 = A_COLS + 2 * B_WIDTH
    for j in range(2):
        scale = C_DH ** -0.5 if j == 0 else 1.0
        lo, hi = halves(project(c0 + j * MXU_WIDTH))
        c_ref[:, j * MXU_WIDTH:j * MXU_WIDTH + LANES] = rope(lo, scale)
        c_ref[:, j * MXU_WIDTH + LANES:(j + 1) * MXU_WIDTH] = rope(hi, scale)
    c_ref[:, 2 * MXU_WIDTH:] = project(c0 + 2 * MXU_WIDTH).astype(BF16)


def _inproj(x, w_in, cos_t, sin_t, ln_g, ln_b, sgu_w, sgu_bias):
    t = x.shape[0]
    rows = lambda n: pl.BlockSpec((ROW_TILE, n), lambda i: (i, 0))
    return pl.pallas_call(
        _inproj_kernel,
        grid=(t // ROW_TILE,),
        in_specs=[rows(D_MODEL), _resident((D_MODEL, IN_COLS)), rows(LANES), rows(LANES),
                  _resident((1, B_WIDTH)), _resident((1, B_WIDTH)),
                  _resident((B_GROUPS, B_CHUNK, B_CHUNK)), _resident((B_CHUNK, B_WIDTH))],
        out_specs=[rows(A_COLS), rows(B_WIDTH), rows(C_COLS)],
        out_shape=[jax.ShapeDtypeStruct((t, A_COLS), BF16),
                   jax.ShapeDtypeStruct((t, B_WIDTH), BF16),
                   jax.ShapeDtypeStruct((t, C_COLS), BF16)],
        compiler_params=_params("parallel"),
        name="mixer_inproj",
    )(x, w_in, cos_t, sin_t, ln_g, ln_b, sgu_w, sgu_bias)


def _two_map_step(q_lo, q_hi, k_blk, v_blk, stats, masks, first):
    v_b = v_blk
    for q_m, (m_ref, l_ref, acc_ref), mask in zip((q_lo, q_hi), stats, masks):
        s = _dot_nt(q_m, k_blk)
        if mask is not None:
            s = mask(s)
        blk_max = jnp.max(s, axis=-1, keepdims=True)
        if first:
            m_new = blk_max
            p = jnp.exp(s - m_new)
            l_ref[...] = jnp.sum(p, axis=-1, keepdims=True)
            acc_ref[...] = _dot(p.astype(BF16), v_b)
        else:
            m_old = m_ref[...]
            m_new = jnp.maximum(m_old, blk_max)
            p = jnp.exp(s - m_new)
            alpha = jnp.exp(m_old - m_new)
            l_ref[...] = alpha * l_ref[...] + jnp.sum(p, axis=-1, keepdims=True)
            acc_ref[...] = alpha * acc_ref[...] + _dot(p.astype(BF16), v_b)
        m_ref[...] = m_new


def _split_maps(q):
    lane = lax.broadcasted_iota(jnp.int32, (1, LANES), 1)
    low = lane < LANES // 2
    zero = jnp.zeros((), q.dtype)
    return low, jnp.where(low, q, zero), jnp.where(low, zero, q)


def _diff_attn_kernel(lam_ref, q_ref, k_ref, v_ref, g_ref, o_ref,
                      m1, l1, acc1, m2, l2, acc2, *, lam_init):
    qi = pl.program_id(2)
    tq = ATTN_TILE
    _, q1, q2 = _split_maps(q_ref[...])
    stats = ((m1, l1, acc1), (m2, l2, acc2))

    row = lax.broadcasted_iota(jnp.int32, (tq, tq), 0)
    col = lax.broadcasted_iota(jnp.int32, (tq, tq), 1)
    causal = lambda s: jnp.where(col <= row, s, NEG_INF)
    d0 = pl.multiple_of(qi * tq, tq)
    _two_map_step(q1, q2, k_ref[pl.ds(d0, tq), :], v_ref[pl.ds(d0, tq), :],
                  stats, (causal, causal), first=True)

    def past(kb, carry):
        k0 = pl.multiple_of(kb * tq, tq)
        _two_map_step(q1, q2, k_ref[pl.ds(k0, tq), :], v_ref[pl.ds(k0, tq), :],
                      stats, (None, None), first=False)
        return carry

    lax.fori_loop(0, qi, past, 0)

    lp = lam_ref[...]
    lam = (jnp.exp(jnp.sum(lp[0:1] * lp[1:2], axis=-1, keepdims=True))
           - jnp.exp(jnp.sum(lp[2:3] * lp[3:4], axis=-1, keepdims=True)) + lam_init)
    o = acc1[...] / l1[...] - lam * (acc2[...] / l2[...])
    ms = jnp.mean(o * o, axis=-1, keepdims=True)
    o_ref[...] = (o * lax.rsqrt(ms + LN_EPS) * g_ref[...] * (1.0 - lam_init)).astype(BF16)


def _diff_attn(qkv, lam_params, subln_g, lam_init):
    bsz, seq, _ = qkv.shape
    tq = ATTN_TILE
    kv = lambda off: pl.BlockSpec((None, seq, LANES), lambda b, h, i: (b, 0, off + h))
    stat = lambda n: pltpu.VMEM((tq, n), F32)
    return pl.pallas_call(
        functools.partial(_diff_attn_kernel, lam_init=lam_init),
        grid=(bsz, A_HEADS, seq // tq),
        in_specs=[pl.BlockSpec((4, A_DH), lambda b, h, i: (0, 0)),
                  pl.BlockSpec((None, tq, LANES), lambda b, h, i: (b, i, h)),
                  kv(A_HEADS), kv(2 * A_HEADS),
                  pl.BlockSpec((1, A_DV), lambda b, h, i: (0, 0))],
        out_specs=pl.BlockSpec((None, tq, LANES), lambda b, h, i: (b, i, h)),
        out_shape=jax.ShapeDtypeStruct((bsz, seq, A_HEADS * A_DV), BF16),
        scratch_shapes=[stat(1), stat(1), stat(A_DV), stat(1), stat(1), stat(A_DV)],
        compiler_params=_params("parallel", "parallel", "arbitrary"),
        name="diff_attn",
    )(lam_params, qkv, qkv, qkv, subln_g)


def _moba_kernel(q_ref, k_ref, v_ref, o_ref, kbar_ref, ma, la, acca, mb, lb, accb):
    qi = pl.program_id(2)
    blk = C_BLOCK
    n_blocks = k_ref.shape[0] // blk

    @pl.when(qi == 0)
    def _():
        for n in range(n_blocks):
            kn = k_ref[n * blk:(n + 1) * blk, :].astype(F32)
            kbar_ref[n:n + 1, :] = jnp.mean(kn, axis=0, keepdims=True)

    low, qa, qb = _split_maps(q_ref[...])
    stats = ((ma, la, acca), (mb, lb, accb))

    row = lax.broadcasted_iota(jnp.int32, (blk, blk), 0)
    col = lax.broadcasted_iota(jnp.int32, (blk, blk), 1)
    causal = lambda s: jnp.where(col <= row, s, NEG_INF)
    d0 = pl.multiple_of(qi * blk, blk)
    _two_map_step(qa, qb, k_ref[pl.ds(d0, blk), :], v_ref[pl.ds(d0, blk), :],
                  stats, (causal, causal), first=True)

    kbar = kbar_ref[...].astype(BF16)
    blk_id = lax.broadcasted_iota(jnp.int32, (1, n_blocks), 1)
    is_past = blk_id < qi
    bias = []
    for q_m in (qa, qb):
        gate = _dot_nt(q_m, kbar)
        per_block = []
        for n in range(n_blocks - 1):
            g_n = gate[:, n:n + 1]
            ahead = (gate > g_n) | ((gate == g_n) & (blk_id < n))
            rank = jnp.sum(jnp.where(ahead & is_past, 1.0, 0.0), axis=-1, keepdims=True)
            per_block.append(jnp.where(rank < C_TOPK, 0.0, NEG_INF))
        bias.append(per_block)

    for n in range(n_blocks - 1):
        @pl.when(n < qi)
        def _():
            add = lambda b: (lambda s: s + b)
            _two_map_step(qa, qb, k_ref[n * blk:(n + 1) * blk, :], v_ref[n * blk:(n + 1) * blk, :],
                          stats, (add(bias[0][n]), add(bias[1][n])), first=False)

    o = jnp.where(low, acca[...] / la[...], accb[...] / lb[...])
    o_ref[...] = o.astype(BF16)


def _moba(qkv):
    bsz, seq, _ = qkv.shape
    blk = C_BLOCK
    pairs = C_HEADS * C_DH // LANES
    kv = lambda off: pl.BlockSpec((None, seq, LANES), lambda b, h, i: (b, 0, off + h))
    stat = lambda n: pltpu.VMEM((blk, n), F32)
    return pl.pallas_call(
        _moba_kernel,
        grid=(bsz, pairs, seq // blk),
        in_specs=[pl.BlockSpec((None, blk, LANES), lambda b, h, i: (b, i, h)),
                  kv(pairs), kv(2 * pairs)],
        out_specs=pl.BlockSpec((None, blk, LANES), lambda b, h, i: (b, i, h)),
        out_shape=jax.ShapeDtypeStruct((bsz, seq, C_HEADS * C_DH), BF16),
        scratch_shapes=[pltpu.VMEM((seq // blk, LANES), F32),
                        stat(1), stat(1), stat(LANES), stat(1), stat(1), stat(LANES)],
        compiler_params=_params("parallel", "parallel", "arbitrary"),
        name="moba_attn",
    )(qkv, qkv, qkv)


def _outproj_ln_kernel(x_ref, oa_ref, ob_ref, oc_ref, w_ref, g_ref, b_ref, o_ref):
    na = A_HEADS * A_DV
    y = (_dot(oa_ref[...], w_ref[:na, :])
         + _dot(ob_ref[...], w_ref[na:na + B_WIDTH, :])
         + _dot(oc_ref[...], w_ref[na + B_WIDTH:, :]))
    o_ref[...] = _layer_norm(DEEPNORM_ALPHA * x_ref[...] + y, g_ref[...], b_ref[...])


def _outproj_ln(x, oa, ob, oc, w_out, g, b):
    t = x.shape[0]
    rows = lambda n: pl.BlockSpec((ROW_TILE, n), lambda i: (i, 0))
    return pl.pallas_call(
        _outproj_ln_kernel,
        grid=(t // ROW_TILE,),
        in_specs=[rows(D_MODEL), rows(A_HEADS * A_DV), rows(B_WIDTH), rows(C_HEADS * C_DH),
                  _resident((D_MODEL, D_MODEL)), _resident((1, D_MODEL)), _resident((1, D_MODEL))],
        out_specs=rows(D_MODEL),
        out_shape=jax.ShapeDtypeStruct((t, D_MODEL), F32),
        compiler_params=_params("parallel"),
        name="mixer_outproj_ln",
    )(x, oa, ob, oc, w_out, g, b)


def _mem_kv_kernel(mem_ref, wk_ref, wv_ref, k_ref, v_ref):
    mb = mem_ref[...].astype(BF16)
    k_ref[...] = _dot(mb, wk_ref[...]).astype(BF16)
    v_ref[...] = _dot(mb, wv_ref[...]).astype(BF16)


def _mem_kv(mem2d, wk, wv):
    rows = mem2d.shape[0]
    w = pl.BlockSpec((None, D_MODEL, D_MODEL), lambda l, i: (l, 0, 0))
    out = pl.BlockSpec((None, ROW_TILE, D_MODEL), lambda l, i: (l, i, 0))
    shape = jax.ShapeDtypeStruct((DEPTH, rows, D_MODEL), BF16)
    return pl.pallas_call(
        _mem_kv_kernel,
        grid=(DEPTH, rows // ROW_TILE),
        in_specs=[pl.BlockSpec((ROW_TILE, D_MODEL), lambda l, i: (i, 0)), w, w],
        out_specs=[out, out],
        out_shape=[shape, shape],
        compiler_params=_params("parallel", "parallel"),
        name="mem_kv_proj",
    )(mem2d, wk, wv)


def _xattn_ln_kernel(x_ref, k_ref, v_ref, wq_ref, wo_ref, g_ref, b_ref, o_ref, cat_ref):
    x = x_ref[...]
    q = (_dot(x.astype(BF16), wq_ref[...]) * XA_DH ** -0.5).astype(BF16)
    for h in range(XA_HEADS):
        sl = slice(h * XA_DH, (h + 1) * XA_DH)
        s = _dot_nt(q[:, sl], k_ref[:, sl])
        p = jnp.exp(s - jnp.max(s, axis=-1, keepdims=True))
        denom = jnp.sum(p, axis=-1, keepdims=True)
        cat_ref[:, sl] = (_dot(p.astype(BF16), v_ref[:, sl]) / denom).astype(BF16)
    y = _dot(cat_ref[...], wo_ref[...])
    o_ref[...] = _layer_norm(DEEPNORM_ALPHA * x + y, g_ref[...], b_ref[...])


def _xattn_ln(x3, k, v, wq, wo, g, b):
    bsz, seq, _ = x3.shape
    rows = pl.BlockSpec((None, ROW_TILE, D_MODEL), lambda b_, i: (b_, i, 0))
    kv = pl.BlockSpec((None, MEM_LEN, D_MODEL), lambda b_, i: (b_, 0, 0))
    return pl.pallas_call(
        _xattn_ln_kernel,
        grid=(bsz, seq // ROW_TILE),
        in_specs=[rows, kv, kv, _resident((D_MODEL, D_MODEL)), _resident((D_MODEL, D_MODEL)),
                  _resident((1, D_MODEL)), _resident((1, D_MODEL))],
        out_specs=rows,
        out_shape=jax.ShapeDtypeStruct((bsz, seq, D_MODEL), F32),
        scratch_shapes=[pltpu.VMEM((ROW_TILE, D_MODEL), BF16)],
        compiler_params=_params("parallel", "parallel"),
        name="xattn_ln",
    )(x3, k, v, wq, wo, g, b)


def kernel(x, mem, positions, ffn1_w_gate, ffn1_w_up, ffn1_w_down, ln1_g, ln1_b, mix_w_in, diff_lq1, diff_lk1, diff_lq2, diff_lk2, diff_subln_g, sgu_ln_g, sgu_ln_b, sgu_w, sgu_b, mix_w_out, ln2_g, ln2_b, xa_wq, xa_wk, xa_wv, xa_wo, ln3_g, ln3_b, ffn2_w_gate, ffn2_w_up, ffn2_w_down, ln4_g, ln4_b):
    bsz, seq, d = x.shape
    t = bsz * seq
    bf = lambda w: w.astype(BF16)
    row = lambda p, l: p[l].reshape(1, -1)

    cos_t, sin_t = _rope_tables(positions)
    mem_k, mem_v = _mem_kv(mem.reshape(bsz * MEM_LEN, d), bf(xa_wk), bf(xa_wv))
    sgu_bias = jnp.repeat(jnp.swapaxes(sgu_b, 1, 2), B_DG, axis=2)
    lam_params = jnp.stack([diff_lq1, diff_lk1, diff_lq2, diff_lk2], axis=1)

    h = x.reshape(t, d)
    for l in range(DEPTH):
        lam_init = 0.8 - 0.6 * math.exp(-0.3 * l)
        h = _ffn_ln(h, bf(ffn1_w_gate[l]), bf(ffn1_w_up[l]), bf(ffn1_w_down[l]),
                    row(ln1_g, l), row(ln1_b, l))
        qkv_a, ob, qkv_c = _inproj(h, bf(mix_w_in[l]), cos_t, sin_t, row(sgu_ln_g, l),
                                   row(sgu_ln_b, l), sgu_w[l], sgu_bias[l])
        oa = _diff_attn(qkv_a.reshape(bsz, seq, A_COLS), lam_params[l], row(diff_subln_g, l), lam_init)
        oc = _moba(qkv_c.reshape(bsz, seq, C_COLS))
        h = _outproj_ln(h, oa.reshape(t, -1), ob, oc.reshape(t, -1), bf(mix_w_out[l]),
                        row(ln2_g, l), row(ln2_b, l))
        h = _xattn_ln(h.reshape(bsz, seq, d),
                      mem_k[l].reshape(bsz, MEM_LEN, d), mem_v[l].reshape(bsz, MEM_LEN, d),
                      bf(xa_wq[l]), bf(xa_wo[l]), row(ln3_g, l), row(ln3_b, l)).reshape(t, d)
        h = _ffn_ln(h, bf(ffn2_w_gate[l]), bf(ffn2_w_up[l]), bf(ffn2_w_down[l]),
                    row(ln4_g, l), row(ln4_b, l))
    return h.reshape(bsz, seq, d)
```

```python
import functools
import math

import jax
import jax.numpy as jnp
from jax import lax
from jax.experimental import pallas as pl
from jax.experimental.pallas import tpu as pltpu

D_MODEL = 1024
DEPTH = 2
MEM_LEN = 256
A_HEADS = 4
A_DH = 64
A_DV = 2 * A_DH
B_GROUPS = 4
B_DG = 64
B_WIDTH = B_GROUPS * B_DG
B_CHUNK = 128
C_HEADS = 4
C_DH = 64
C_BLOCK = 256
C_TOPK = 3
A_COLS = 3 * A_HEADS * A_DV
C_COLS = 3 * C_HEADS * C_DH
IN_COLS = A_COLS + 2 * B_WIDTH + C_COLS
XA_HEADS = 4
XA_DH = D_MODEL // XA_HEADS
D_FF = 2816
ROPE_THETA = 10000.0
LN_EPS = 1e-5
DEEPNORM_ALPHA = (2.0 * DEPTH) ** 0.25

LANES = 128
MXU_WIDTH = 256
VMEM_LIMIT_BYTES = 56 * 1024 * 1024

ROW_TILE = 512
FF_CHUNK = MXU_WIDTH
ATTN_TILE = 256

F32 = jnp.float32
BF16 = jnp.bfloat16
NEG_INF = float("-inf")


def _params(*sem):
    return pltpu.CompilerParams(dimension_semantics=sem, vmem_limit_bytes=VMEM_LIMIT_BYTES)


def _resident(shape):
    zeros = (0,) * len(shape)
    return pl.BlockSpec(shape, lambda *_: zeros, pipeline_mode=pl.Buffered(1))


def _layer_norm(y, g, b):
    mu = jnp.mean(y, axis=-1, keepdims=True)
    d = y - mu
    var = jnp.mean(d * d, axis=-1, keepdims=True)
    return d * lax.rsqrt(var + LN_EPS) * g + b


def _dot(a, b):
    return jnp.dot(a, b, preferred_element_type=F32)


def _dot_nt(a, b):
    return lax.dot_general(a, b, (((1,), (1,)), ((), ())), preferred_element_type=F32)


def _rope_kernel(pos_ref, invf_ref, cos_ref, sin_ref):
    ang = pos_ref[...] * invf_ref[...]
    lane = lax.broadcasted_iota(jnp.int32, (1, LANES), 1)
    sign = jnp.where(lane % A_DH < A_DH // 2, -1.0, 1.0).astype(F32)
    cos_ref[...] = jnp.cos(ang)
    sin_ref[...] = jnp.sin(ang) * sign


def _rope_tables(positions):
    t = positions.size
    half = A_DH // 2
    inv_freq = 1.0 / (ROPE_THETA ** (jnp.arange(0, A_DH, 2, dtype=F32) / A_DH))
    invf = jnp.tile(inv_freq, LANES // half).reshape(1, LANES)
    pos = positions.astype(F32).reshape(t, 1)
    tm = 2048
    out = jax.ShapeDtypeStruct((t, LANES), F32)
    return pl.pallas_call(
        _rope_kernel,
        grid=(t // tm,),
        in_specs=[pl.BlockSpec((tm, 1), lambda i: (i, 0)),
                  pl.BlockSpec((1, LANES), lambda i: (0, 0))],
        out_specs=[pl.BlockSpec((tm, LANES), lambda i: (i, 0))] * 2,
        out_shape=[out, out],
        compiler_params=_params("parallel"),
        name="rope_tables",
    )(pos, invf)


def _ffn_ln_kernel(x_ref, wg_ref, wu_ref, wd_ref, g_ref, b_ref, o_ref, acc_ref):
    x = x_ref[...]
    xb = x.astype(BF16)
    for c in range(D_FF // FF_CHUNK):
        cols = slice(c * FF_CHUNK, (c + 1) * FF_CHUNK)
        gate = _dot(xb, wg_ref[:, cols])
        up = _dot(xb, wu_ref[:, cols])
        act = (gate * jax.nn.sigmoid(gate) * up).astype(BF16)
        part = _dot(act, wd_ref[cols, :])
        if c == 0:
            acc_ref[...] = part
        else:
            acc_ref[...] += part
    y = DEEPNORM_ALPHA * x + 0.5 * acc_ref[...]
    o_ref[...] = _layer_norm(y, g_ref[...], b_ref[...])


def _ffn_ln(x, wg, wu, wd, g, b):
    t = x.shape[0]
    row = pl.BlockSpec((ROW_TILE, D_MODEL), lambda i: (i, 0))
    return pl.pallas_call(
        _ffn_ln_kernel,
        grid=(t // ROW_TILE,),
        in_specs=[row, _resident((D_MODEL, D_FF)), _resident((D_MODEL, D_FF)),
                  _resident((D_FF, D_MODEL)), _resident((1, D_MODEL)), _resident((1, D_MODEL))],
        out_specs=row,
        out_shape=jax.ShapeDtypeStruct((t, D_MODEL), F32),
        scratch_shapes=[pltpu.VMEM((ROW_TILE, D_MODEL), F32)],
        compiler_params=_params("parallel"),
        name="ffn_ln",
    )(x, wg, wu, wd, g, b)


def _gelu(y):
    return 0.5 * y * (1.0 + lax.erf(y * (2.0 ** -0.5)))


def _inproj_kernel(x_ref, w_ref, cos_ref, sin_ref, lng_ref, lnb_ref, sw_ref, sb_ref,
                   a_ref, ob_ref, c_ref):
    xb = x_ref[...].astype(BF16)
    cos = cos_ref[...]
    sin = sin_ref[...]
    lane = lax.broadcasted_iota(jnp.int32, (1, LANES), 1)
    first_half = lane % A_DH < A_DH // 2
    low_lanes = lane < B_DG

    def rope(y, scale):
        swapped = jnp.where(first_half, pltpu.roll(y, LANES - A_DH // 2, 1),
                            pltpu.roll(y, A_DH // 2, 1))
        return ((y * cos + swapped * sin) * scale).astype(BF16)

    def project(col):
        return _dot(xb, w_ref[:, col:col + MXU_WIDTH])

    def halves(y):
        return y[:, :LANES], y[:, LANES:]

    q_scale = A_DH ** -0.5
    for j in range(2 * A_HEADS * A_DV // MXU_WIDTH):
        col = j * MXU_WIDTH
        scale = q_scale if col < A_HEADS * A_DV else 1.0
        lo, hi = halves(project(col))
        a_ref[:, col:col + LANES] = rope(lo, scale)
        a_ref[:, col + LANES:col + MXU_WIDTH] = rope(hi, scale)
    for j in range(A_HEADS * A_DV // MXU_WIDTH):
        col = 2 * A_HEADS * A_DV + j * MXU_WIDTH
        a_ref[:, col:col + MXU_WIDTH] = project(col).astype(BF16)

    u = _gelu(project(A_COLS))
    v = _gelu(project(A_COLS + B_WIDTH))
    vn = _layer_norm(v, lng_ref[...], lnb_ref[...]).astype(BF16)
    rows = lax.broadcasted_iota(jnp.int32, (B_CHUNK, B_CHUNK), 0)
    cols = lax.broadcasted_iota(jnp.int32, (B_CHUNK, B_CHUNK), 1)
    w_tril = [jnp.where(rows >= cols, sw_ref[g], 0.0).astype(BF16) for g in range(B_GROUPS)]
    zero = jnp.zeros((), BF16)
    for ci in range(ROW_TILE // B_CHUNK):
        rsl = slice(ci * B_CHUNK, (ci + 1) * B_CHUNK)
        for hb in range(B_WIDTH // LANES):
            csl = slice(hb * LANES, (hb + 1) * LANES)
            vblk = vn[rsl, csl]
            mix = (_dot(w_tril[2 * hb], jnp.where(low_lanes, vblk, zero))
                   + _dot(w_tril[2 * hb + 1], jnp.where(low_lanes, zero, vblk))
                   + sb_ref[:, csl])
            ob_ref[rsl, csl] = (u[rsl, csl] * mix).astype(BF16)

    c0 = A_COLS + 2 * B_WIDTH
    for j in range(2):
        scale = C_DH ** -0.5 if j == 0 else 1.0
        lo, hi = halves(project(c0 + j * MXU_WIDTH))
        c_ref[:, j * MXU_WIDTH:j * MXU_WIDTH + LANES] = rope(lo, scale)
        c_ref[:, j * MXU_WIDTH + LANES:(j + 1) * MXU_WIDTH] = rope(hi, scale)
    c_ref[:, 2 * MXU_WIDTH:] = project(c0 + 2 * MXU_WIDTH).astype(BF16)


def _inproj(x, w_in, cos_t, sin_t, ln_g, ln_b, sgu_w, sgu_bias):
    t = x.shape[0]
    rows = lambda n: pl.BlockSpec((ROW_TILE, n), lambda i: (i, 0))
    return pl.pallas_call(
        _inproj_kernel,
        grid=(t // ROW_TILE,),
        in_specs=[rows(D_MODEL), _resident((D_MODEL, IN_COLS)), rows(LANES), rows(LANES),
                  _resident((1, B_WIDTH)), _resident((1, B_WIDTH)),
                  _resident((B_GROUPS, B_CHUNK, B_CHUNK)), _resident((B_CHUNK, B_WIDTH))],
        out_specs=[rows(A_COLS), rows(B_WIDTH), rows(C_COLS)],
        out_shape=[jax.ShapeDtypeStruct((t, A_COLS), BF16),
                   jax.ShapeDtypeStruct((t, B_WIDTH), BF16),
                   jax.ShapeDtypeStruct((t, C_COLS), BF16)],
        compiler_params=_params("parallel"),
        name="mixer_inproj",
    )(x, w_in, cos_t, sin_t, ln_g, ln_b, sgu_w, sgu_bias)


def _attend_tile(q_past, q_diag, k_ref, v_ref, i, tile, causal):
    d0 = i * tile
    s_d = jnp.where(causal, _dot_nt(q_diag, k_ref[d0:d0 + tile, :]), NEG_INF)
    m = jnp.max(s_d, axis=-1, keepdims=True)
    if i > 0:
        s_p = _dot_nt(q_past, k_ref[0:d0, :])
        m = jnp.maximum(m, jnp.max(s_p, axis=-1, keepdims=True))
    p_d = jnp.exp(s_d - m)
    denom = jnp.sum(p_d, axis=-1, keepdims=True)
    acc = _dot(p_d.astype(BF16), v_ref[d0:d0 + tile, :])
    if i > 0:
        p_p = jnp.exp(s_p - m)
        denom = denom + jnp.sum(p_p, axis=-1, keepdims=True)
        acc = acc + _dot(p_p.astype(BF16), v_ref[0:d0, :])
    return acc, denom


def _causal_mask(tile):
    row = lax.broadcasted_iota(jnp.int32, (tile, tile), 0)
    col = lax.broadcasted_iota(jnp.int32, (tile, tile), 1)
    return col <= row


def _split_maps(q):
    lane = lax.broadcasted_iota(jnp.int32, (1, LANES), 1)
    low = lane < LANES // 2
    zero = jnp.zeros((), q.dtype)
    return low, jnp.where(low, q, zero), jnp.where(low, zero, q)


def _diff_attn_kernel(lam_ref, q_ref, k_ref, v_ref, g_ref, o_ref, *, lam_init):
    tile = ATTN_TILE
    causal = _causal_mask(tile)
    lp = lam_ref[...]
    lam = (jnp.exp(jnp.sum(lp[0:1] * lp[1:2], axis=-1, keepdims=True))
           - jnp.exp(jnp.sum(lp[2:3] * lp[3:4], axis=-1, keepdims=True)) + lam_init)
    gain = g_ref[...] * (1.0 - lam_init)
    for i in range(q_ref.shape[0] // tile):
        _, q1, q2 = _split_maps(q_ref[i * tile:(i + 1) * tile, :])
        acc1, l1 = _attend_tile(q1, q1, k_ref, v_ref, i, tile, causal)
        acc2, l2 = _attend_tile(q2, q2, k_ref, v_ref, i, tile, causal)
        o = acc1 / l1 - lam * (acc2 / l2)
        ms = jnp.mean(o * o, axis=-1, keepdims=True)
        o_ref[i * tile:(i + 1) * tile, :] = (o * lax.rsqrt(ms + LN_EPS) * gain).astype(BF16)


def _diff_attn(qkv, lam_params, subln_g, lam_init):
    bsz, seq, _ = qkv.shape
    head = lambda off: pl.BlockSpec((None, seq, LANES), lambda b, h: (b, 0, off + h))
    return pl.pallas_call(
        functools.partial(_diff_attn_kernel, lam_init=lam_init),
        grid=(bsz, A_HEADS),
        in_specs=[pl.BlockSpec((4, A_DH), lambda b, h: (0, 0)),
                  head(0), head(A_HEADS), head(2 * A_HEADS),
                  pl.BlockSpec((1, A_DV), lambda b, h: (0, 0))],
        out_specs=head(0),
        out_shape=jax.ShapeDtypeStruct((bsz, seq, A_HEADS * A_DV), BF16),
        compiler_params=_params("parallel", "parallel"),
        name="diff_attn",
    )(lam_params, qkv, qkv, qkv, subln_g)


MASKED_SCORE = -1e30


def _moba_kernel(q_ref, k_ref, v_ref, o_ref, kaug_ref):
    blk = C_BLOCK
    seq = q_ref.shape[0]
    n_blocks = seq // blk
    causal = _causal_mask(blk)
    lane = lax.broadcasted_iota(jnp.int32, (1, LANES), 1)
    low = lane < LANES // 2

    k = k_ref[...]
    key_block = lax.broadcasted_iota(jnp.int32, (seq, LANES), 0) // blk
    lane_full = lax.broadcasted_iota(jnp.int32, (seq, LANES), 1)
    pen_lane0 = (LANES // 2, 0)
    for hd in range(2):
        onehot = jnp.where(lane_full - pen_lane0[hd] == key_block, 1.0, 0.0).astype(BF16)
        own = low if hd == 0 else jnp.logical_not(low)
        kaug_ref[hd] = jnp.where(own, k, onehot)
    kbar = jnp.mean(k.astype(F32).reshape(n_blocks, blk, LANES), axis=1).astype(BF16)

    blk_row = lax.broadcasted_iota(jnp.int32, (n_blocks, LANES), 0)
    blk_lane = lax.broadcasted_iota(jnp.int32, (n_blocks, LANES), 1)

    for i in range(n_blocks):
        _, qa, qb = _split_maps(q_ref[i * blk:(i + 1) * blk, :])
        outs = []
        for hd, q_m in enumerate((qa, qb)):
            q_past = q_m
            if i > C_TOPK:
                gate = _dot_nt(kbar, q_m)
                rows = [gate[n:n + 1, :] for n in range(i)]
                pen_rows = []
                for n in range(i):
                    ahead = jnp.zeros_like(rows[n])
                    for m_ in range(i):
                        if m_ < n:
                            ahead = ahead + jnp.where(rows[m_] >= rows[n], 1.0, 0.0)
                        elif m_ > n:
                            ahead = ahead + jnp.where(rows[m_] > rows[n], 1.0, 0.0)
                    pen_rows.append(jnp.where(ahead < C_TOPK, 0.0, MASKED_SCORE))
                pen_rows += [jnp.zeros_like(rows[0])] * (n_blocks - i)
                pen_t = jnp.concatenate(pen_rows, axis=0).astype(BF16)
                place = jnp.where(blk_lane - pen_lane0[hd] == blk_row, 1.0, 0.0).astype(BF16)
                placed = lax.dot_general(pen_t, place, (((0,), (0,)), ((), ())),
                                         preferred_element_type=F32)
                q_past = q_m + placed.astype(BF16)
            acc, denom = _attend_tile(q_past, q_m, kaug_ref.at[hd], v_ref, i, blk, causal)
            outs.append(acc / denom)
        o_ref[i * blk:(i + 1) * blk, :] = jnp.where(low, outs[0], outs[1]).astype(BF16)


def _moba(qkv):
    bsz, seq, _ = qkv.shape
    pairs = C_HEADS * C_DH // LANES
    head = lambda off: pl.BlockSpec((None, seq, LANES), lambda b, h: (b, 0, off + h))
    return pl.pallas_call(
        _moba_kernel,
        grid=(bsz, pairs),
        in_specs=[head(0), head(pairs), head(2 * pairs)],
        out_specs=head(0),
        out_shape=jax.ShapeDtypeStruct((bsz, seq, C_HEADS * C_DH), BF16),
        scratch_shapes=[pltpu.VMEM((2, seq, LANES), BF16)],
        compiler_params=_params("parallel", "parallel"),
        name="moba_attn",
    )(qkv, qkv, qkv)


def _outproj_ln_kernel(x_ref, oa_ref, ob_ref, oc_ref, w_ref, g_ref, b_ref, o_ref):
    na = A_HEADS * A_DV
    y = (_dot(oa_ref[...], w_ref[:na, :])
         + _dot(ob_ref[...], w_ref[na:na + B_WIDTH, :])
         + _dot(oc_ref[...], w_ref[na + B_WIDTH:, :]))
    o_ref[...] = _layer_norm(DEEPNORM_ALPHA * x_ref[...] + y, g_ref[...], b_ref[...])


def _outproj_ln(x, oa, ob, oc, w_out, g, b):
    t = x.shape[0]
    rows = lambda n: pl.BlockSpec((ROW_TILE, n), lambda i: (i, 0))
    return pl.pallas_call(
        _outproj_ln_kernel,
        grid=(t // ROW_TILE,),
        in_specs=[rows(D_MODEL), rows(A_HEADS * A_DV), rows(B_WIDTH), rows(C_HEADS * C_DH),
                  _resident((D_MODEL, D_MODEL)), _resident((1, D_MODEL)), _resident((1, D_MODEL))],
        out_specs=rows(D_MODEL),
        out_shape=jax.ShapeDtypeStruct((t, D_MODEL), F32),
        compiler_params=_params("parallel"),
        name="mixer_outproj_ln",
    )(x, oa, ob, oc, w_out, g, b)


def _mem_kv_kernel(mem_ref, wk_ref, wv_ref, k_ref, v_ref):
    mb = mem_ref[...].astype(BF16)
    k_ref[...] = _dot(mb, wk_ref[...]).astype(BF16)
    v_ref[...] = _dot(mb, wv_ref[...]).astype(BF16)


def _mem_kv(mem2d, wk, wv):
    rows = mem2d.shape[0]
    w = pl.BlockSpec((None, D_MODEL, D_MODEL), lambda l, i: (l, 0, 0))
    out = pl.BlockSpec((None, ROW_TILE, D_MODEL), lambda l, i: (l, i, 0))
    shape = jax.ShapeDtypeStruct((DEPTH, rows, D_MODEL), BF16)
    return pl.pallas_call(
        _mem_kv_kernel,
        grid=(DEPTH, rows // ROW_TILE),
        in_specs=[pl.BlockSpec((ROW_TILE, D_MODEL), lambda l, i: (i, 0)), w, w],
        out_specs=[out, out],
        out_shape=[shape, shape],
        compiler_params=_params("parallel", "parallel"),
        name="mem_kv_proj",
    )(mem2d, wk, wv)


def _xattn_ln_kernel(x_ref, k_ref, v_ref, wq_ref, wo_ref, g_ref, b_ref, o_ref, cat_ref):
    x = x_ref[...]
    q = (_dot(x.astype(BF16), wq_ref[...]) * XA_DH ** -0.5).astype(BF16)
    for h in range(XA_HEADS):
        sl = slice(h * XA_DH, (h + 1) * XA_DH)
        s = _dot_nt(q[:, sl], k_ref[:, sl])
        p = jnp.exp(s - jnp.max(s, axis=-1, keepdims=True))
        denom = jnp.sum(p, axis=-1, keepdims=True)
        cat_ref[:, sl] = (_dot(p.astype(BF16), v_ref[:, sl]) / denom).astype(BF16)
    y = _dot(cat_ref[...], wo_ref[...])
    o_ref[...] = _layer_norm(DEEPNORM_ALPHA * x + y, g_ref[...], b_ref[...])


def _xattn_ln(x3, k, v, wq, wo, g, b):
    bsz, seq, _ = x3.shape
    rows = pl.BlockSpec((None, ROW_TILE, D_MODEL), lambda b_, i: (b_, i, 0))
    kv = pl.BlockSpec((None, MEM_LEN, D_MODEL), lambda b_, i: (b_, 0, 0))
    return pl.pallas_call(
        _xattn_ln_kernel,
        grid=(bsz, seq // ROW_TILE),
        in_specs=[rows, kv, kv, _resident((D_MODEL, D_MODEL)), _resident((D_MODEL, D_MODEL)),
                  _resident((1, D_MODEL)), _resident((1, D_MODEL))],
        out_specs=rows,
        out_shape=jax.ShapeDtypeStruct((bsz, seq, D_MODEL), F32),
        scratch_shapes=[pltpu.VMEM((ROW_TILE, D_MODEL), BF16)],
        compiler_params=_params("parallel", "parallel"),
        name="xattn_ln",
    )(x3, k, v, wq, wo, g, b)


def kernel(x, mem, positions, ffn1_w_gate, ffn1_w_up, ffn1_w_down, ln1_g, ln1_b, mix_w_in, diff_lq1, diff_lk1, diff_lq2, diff_lk2, diff_subln_g, sgu_ln_g, sgu_ln_b, sgu_w, sgu_b, mix_w_out, ln2_g, ln2_b, xa_wq, xa_wk, xa_wv, xa_wo, ln3_g, ln3_b, ffn2_w_gate, ffn2_w_up, ffn2_w_down, ln4_g, ln4_b):
    bsz, seq, d = x.shape
    t = bsz * seq
    bf = lambda w: w.astype(BF16)
    row = lambda p, l: p[l].reshape(1, -1)

    cos_t, sin_t = _rope_tables(positions)
    mem_k, mem_v = _mem_kv(mem.reshape(bsz * MEM_LEN, d), bf(xa_wk), bf(xa_wv))
    sgu_bias = jnp.repeat(jnp.swapaxes(sgu_b, 1, 2), B_DG, axis=2)
    lam_params = jnp.stack([diff_lq1, diff_lk1, diff_lq2, diff_lk2], axis=1)

    h = x.reshape(t, d)
    for l in range(DEPTH):
        lam_init = 0.8 - 0.6 * math.exp(-0.3 * l)
        h = _ffn_ln(h, bf(ffn1_w_gate[l]), bf(ffn1_w_up[l]), bf(ffn1_w_down[l]),
                    row(ln1_g, l), row(ln1_b, l))
        qkv_a, ob, qkv_c = _inproj(h, bf(mix_w_in[l]), cos_t, sin_t, row(sgu_ln_g, l),
                                   row(sgu_ln_b, l), sgu_w[l], sgu_bias[l])
        oa = _diff_attn(qkv_a.reshape(bsz, seq, A_COLS), lam_params[l], row(diff_subln_g, l), lam_init)
        oc = _moba(qkv_c.reshape(bsz, seq, C_COLS))
        h = _outproj_ln(h, oa.reshape(t, -1), ob, oc.reshape(t, -1), bf(mix_w_out[l]),
                        row(ln2_g, l), row(ln2_b, l))
        h = _xattn_ln(h.reshape(bsz, seq, d),
                      mem_k[l].reshape(bsz, MEM_LEN, d), mem_v[l].reshape(bsz, MEM_LEN, d),
                      bf(xa_wq[l]), bf(xa_wo[l]), row(ln3_g, l), row(ln3_b, l)).reshape(t, d)
        h = _ffn_ln(h, bf(ffn2_w_gate[l]), bf(ffn2_w_up[l]), bf(ffn2_w_down[l]),
                    row(ln4_g, l), row(ln4_b, l))
    return h.reshape(bsz, seq, d)
```

```python
import functools
import math

import jax
import jax.numpy as jnp
from jax import lax
from jax.experimental import pallas as pl
from jax.experimental.pallas import tpu as pltpu

D_MODEL = 1024
DEPTH = 2
MEM_LEN = 256
A_HEADS = 4
A_DH = 64
A_DV = 2 * A_DH
B_GROUPS = 4
B_DG = 64
B_WIDTH = B_GROUPS * B_DG
B_CHUNK = 128
C_HEADS = 4
C_DH = 64
C_BLOCK = 256
C_TOPK = 3
A_COLS = 3 * A_HEADS * A_DV
C_COLS = 3 * C_HEADS * C_DH
IN_COLS = A_COLS + 2 * B_WIDTH + C_COLS
XA_HEADS = 4
XA_DH = D_MODEL // XA_HEADS
D_FF = 2816
ROPE_THETA = 10000.0
LN_EPS = 1e-5
DEEPNORM_ALPHA = (2.0 * DEPTH) ** 0.25

LANES = 128
MXU_WIDTH = 256
VMEM_LIMIT_BYTES = 56 * 1024 * 1024

ROW_TILE = 512
FF_CHUNK = MXU_WIDTH
ATTN_TILE = 256

F32 = jnp.float32
BF16 = jnp.bfloat16
NEG_INF = float("-inf")
LOG2_E = math.log2(math.e)


def _params(*sem):
    return pltpu.CompilerParams(dimension_semantics=sem, vmem_limit_bytes=VMEM_LIMIT_BYTES)


def _resident(shape):
    zeros = (0,) * len(shape)
    return pl.BlockSpec(shape, lambda *_: zeros, pipeline_mode=pl.Buffered(1))


def _layer_norm(y, g, b):
    mu = jnp.mean(y, axis=-1, keepdims=True)
    d = y - mu
    var = jnp.mean(d * d, axis=-1, keepdims=True)
    return d * lax.rsqrt(var + LN_EPS) * g + b


def _dot(a, b):
    return jnp.dot(a, b, preferred_element_type=F32)


def _dot_nt(a, b):
    return lax.dot_general(a, b, (((1,), (1,)), ((), ())), preferred_element_type=F32)


def _rope_kernel(pos_ref, invf_ref, cos_ref, sin_ref):
    ang = pos_ref[...] * invf_ref[...]
    lane = lax.broadcasted_iota(jnp.int32, (1, LANES), 1)
    sign = jnp.where(lane % A_DH < A_DH // 2, -1.0, 1.0).astype(F32)
    cos_ref[...] = jnp.cos(ang)
    sin_ref[...] = jnp.sin(ang) * sign


def _rope_tables(positions):
    t = positions.size
    half = A_DH // 2
    inv_freq = 1.0 / (ROPE_THETA ** (jnp.arange(0, A_DH, 2, dtype=F32) / A_DH))
    invf = jnp.tile(inv_freq, LANES // half).reshape(1, LANES)
    pos = positions.astype(F32).reshape(t, 1)
    tm = 2048
    out = jax.ShapeDtypeStruct((t, LANES), F32)
    return pl.pallas_call(
        _rope_kernel,
        grid=(t // tm,),
        in_specs=[pl.BlockSpec((tm, 1), lambda i: (i, 0)),
                  pl.BlockSpec((1, LANES), lambda i: (0, 0))],
        out_specs=[pl.BlockSpec((tm, LANES), lambda i: (i, 0))] * 2,
        out_shape=[out, out],
        compiler_params=_params("parallel"),
        name="rope_tables",
    )(pos, invf)


def _ffn_ln_kernel(x_ref, wg_ref, wu_ref, wd_ref, g_ref, b_ref, o_ref, acc_ref):
    x = x_ref[...]
    xb = x.astype(BF16)
    for c in range(D_FF // FF_CHUNK):
        cols = slice(c * FF_CHUNK, (c + 1) * FF_CHUNK)
        gate = _dot(xb, wg_ref[:, cols])
        up = _dot(xb, wu_ref[:, cols])
        act = (gate * jax.nn.sigmoid(gate) * up).astype(BF16)
        part = _dot(act, wd_ref[cols, :])
        if c == 0:
            acc_ref[...] = part
        else:
            acc_ref[...] += part
    y = DEEPNORM_ALPHA * x + 0.5 * acc_ref[...]
    o_ref[...] = _layer_norm(y, g_ref[...], b_ref[...])


def _ffn_ln(x, wg, wu, wd, g, b):
    t = x.shape[0]
    row = pl.BlockSpec((ROW_TILE, D_MODEL), lambda i: (i, 0))
    return pl.pallas_call(
        _ffn_ln_kernel,
        grid=(t // ROW_TILE,),
        in_specs=[row, _resident((D_MODEL, D_FF)), _resident((D_MODEL, D_FF)),
                  _resident((D_FF, D_MODEL)), _resident((1, D_MODEL)), _resident((1, D_MODEL))],
        out_specs=row,
        out_shape=jax.ShapeDtypeStruct((t, D_MODEL), F32),
        scratch_shapes=[pltpu.VMEM((ROW_TILE, D_MODEL), F32)],
        compiler_params=_params("parallel"),
        name="ffn_ln",
    )(x, wg, wu, wd, g, b)


def _gelu(y):
    return 0.5 * y * (1.0 + lax.erf(y * (2.0 ** -0.5)))


def _inproj_kernel(x_ref, w_ref, cos_ref, sin_ref, lng_ref, lnb_ref, sw_ref, sb_ref,
                   a_ref, ob_ref, c_ref):
    xb = x_ref[...].astype(BF16)
    cos = cos_ref[...]
    sin = sin_ref[...]
    lane = lax.broadcasted_iota(jnp.int32, (1, LANES), 1)
    first_half = lane % A_DH < A_DH // 2
    low_lanes = lane < B_DG

    def rope(y, scale):
        swapped = jnp.where(first_half, pltpu.roll(y, LANES - A_DH // 2, 1),
                            pltpu.roll(y, A_DH // 2, 1))
        return ((y * cos + swapped * sin) * scale).astype(BF16)

    def project(col):
        return _dot(xb, w_ref[:, col:col + MXU_WIDTH])

    def halves(y):
        return y[:, :LANES], y[:, LANES:]

    q_scale = A_DH ** -0.5 * LOG2_E
    for j in range(2 * A_HEADS * A_DV // MXU_WIDTH):
        col = j * MXU_WIDTH
        scale = q_scale if col < A_HEADS * A_DV else 1.0
        lo, hi = halves(project(col))
        a_ref[:, col:col + LANES] = rope(lo, scale)
        a_ref[:, col + LANES:col + MXU_WIDTH] = rope(hi, scale)
    for j in range(A_HEADS * A_DV // MXU_WIDTH):
        col = 2 * A_HEADS * A_DV + j * MXU_WIDTH
        a_ref[:, col:col + MXU_WIDTH] = project(col).astype(BF16)

    u = _gelu(project(A_COLS))
    v = _gelu(project(A_COLS + B_WIDTH))
    vn = _layer_norm(v, lng_ref[...], lnb_ref[...]).astype(BF16)
    rows = lax.broadcasted_iota(jnp.int32, (B_CHUNK, B_CHUNK), 0)
    cols = lax.broadcasted_iota(jnp.int32, (B_CHUNK, B_CHUNK), 1)
    w_tril = [jnp.where(rows >= cols, sw_ref[g], 0.0).astype(BF16) for g in range(B_GROUPS)]
    zero = jnp.zeros((), BF16)
    for ci in range(ROW_TILE // B_CHUNK):
        rsl = slice(ci * B_CHUNK, (ci + 1) * B_CHUNK)
        for hb in range(B_WIDTH // LANES):
            csl = slice(hb * LANES, (hb + 1) * LANES)
            vblk = vn[rsl, csl]
            mix = (_dot(w_tril[2 * hb], jnp.where(low_lanes, vblk, zero))
                   + _dot(w_tril[2 * hb + 1], jnp.where(low_lanes, zero, vblk))
                   + sb_ref[:, csl])
            ob_ref[rsl, csl] = (u[rsl, csl] * mix).astype(BF16)

    c0 = A_COLS + 2 * B_WIDTH
    for j in range(2):
        scale = C_DH ** -0.5 * LOG2_E if j == 0 else 1.0
        lo, hi = halves(project(c0 + j * MXU_WIDTH))
        c_ref[:, j * MXU_WIDTH:j * MXU_WIDTH + LANES] = rope(lo, scale)
        c_ref[:, j * MXU_WIDTH + LANES:(j + 1) * MXU_WIDTH] = rope(hi, scale)
    c_ref[:, 2 * MXU_WIDTH:] = project(c0 + 2 * MXU_WIDTH).astype(BF16)


def _inproj(x, w_in, cos_t, sin_t, ln_g, ln_b, sgu_w, sgu_bias):
    t = x.shape[0]
    rows = lambda n: pl.BlockSpec((ROW_TILE, n), lambda i: (i, 0))
    return pl.pallas_call(
        _inproj_kernel,
        grid=(t // ROW_TILE,),
        in_specs=[rows(D_MODEL), _resident((D_MODEL, IN_COLS)), rows(LANES), rows(LANES),
                  _resident((1, B_WIDTH)), _resident((1, B_WIDTH)),
                  _resident((B_GROUPS, B_CHUNK, B_CHUNK)), _resident((B_CHUNK, B_WIDTH))],
        out_specs=[rows(A_COLS), rows(B_WIDTH), rows(C_COLS)],
        out_shape=[jax.ShapeDtypeStruct((t, A_COLS), BF16),
                   jax.ShapeDtypeStruct((t, B_WIDTH), BF16),
                   jax.ShapeDtypeStruct((t, C_COLS), BF16)],
        compiler_params=_params("parallel"),
        name="mixer_inproj",
    )(x, w_in, cos_t, sin_t, ln_g, ln_b, sgu_w, sgu_bias)


def _tile_scores(q_m, k_ref, i, tile, causal):
    d0 = i * tile
    s = _dot_nt(q_m, k_ref[0:d0 + tile, :])
    s_d = jnp.where(causal, s[:, d0:], NEG_INF)
    m = jnp.max(s_d, axis=-1, keepdims=True)
    if i == 0:
        return [s_d], m
    s_p = s[:, :d0]
    return [s_p, s_d], jnp.maximum(m, jnp.max(s_p, axis=-1, keepdims=True))


def _tile_output(parts, m, v_ref, i, tile):
    p = [jnp.exp2(s - m) for s in parts]
    denom = jnp.sum(p[0], axis=-1, keepdims=True)
    for extra in p[1:]:
        denom = denom + jnp.sum(extra, axis=-1, keepdims=True)
    p = [x.astype(BF16) for x in p]
    p = p[0] if len(p) == 1 else jnp.concatenate(p, axis=1)
    return _dot(p, v_ref[0:(i + 1) * tile, :]) / denom


PIPELINE_AHEAD = 2


def _pipelined(items, score_stage, output_stage):
    outs = {}
    scored = {}
    for j in range(len(items) + PIPELINE_AHEAD):
        if j < len(items):
            scored[items[j]] = score_stage(items[j])
        if j >= PIPELINE_AHEAD:
            item = items[j - PIPELINE_AHEAD]
            outs[item] = output_stage(item, scored.pop(item))
    return outs


def _causal_mask(tile):
    row = lax.broadcasted_iota(jnp.int32, (tile, tile), 0)
    col = lax.broadcasted_iota(jnp.int32, (tile, tile), 1)
    return col <= row


def _low_lanes():
    return lax.broadcasted_iota(jnp.int32, (1, LANES), 1) < LANES // 2


def _map_query(q_ref, i, tile, mp):
    q = q_ref[i * tile:(i + 1) * tile, :]
    keep = _low_lanes() if mp == 0 else jnp.logical_not(_low_lanes())
    return jnp.where(keep, q, jnp.zeros((), q.dtype))


def _diff_attn_kernel(lam_ref, q_ref, k_ref, v_ref, g_ref, o_ref, *, lam_init):
    tile = ATTN_TILE
    n_tiles = q_ref.shape[0] // tile
    causal = _causal_mask(tile)
    lp = lam_ref[...]
    lam = (jnp.exp(jnp.sum(lp[0:1] * lp[1:2], axis=-1, keepdims=True))
           - jnp.exp(jnp.sum(lp[2:3] * lp[3:4], axis=-1, keepdims=True)) + lam_init)
    gain = g_ref[...] * (1.0 - lam_init)

    def scores(item):
        i, mp = item
        return _tile_scores(_map_query(q_ref, i, tile, mp), k_ref, i, tile, causal)

    def output(item, scored):
        return _tile_output(*scored, v_ref, item[0], tile)

    items = [(i, mp) for i in reversed(range(n_tiles)) for mp in range(2)]
    outs = _pipelined(items, scores, output)
    for i in range(n_tiles):
        o = outs[(i, 0)] - lam * outs[(i, 1)]
        ms = jnp.mean(o * o, axis=-1, keepdims=True)
        o_ref[i * tile:(i + 1) * tile, :] = (o * lax.rsqrt(ms + LN_EPS) * gain).astype(BF16)


def _diff_attn(qkv, lam_params, subln_g, lam_init):
    bsz, seq, _ = qkv.shape
    head = lambda off: pl.BlockSpec((None, seq, LANES), lambda b, h: (b, 0, off + h))
    return pl.pallas_call(
        functools.partial(_diff_attn_kernel, lam_init=lam_init),
        grid=(bsz, A_HEADS),
        in_specs=[pl.BlockSpec((4, A_DH), lambda b, h: (0, 0)),
                  head(0), head(A_HEADS), head(2 * A_HEADS),
                  pl.BlockSpec((1, A_DV), lambda b, h: (0, 0))],
        out_specs=head(0),
        out_shape=jax.ShapeDtypeStruct((bsz, seq, A_HEADS * A_DV), BF16),
        compiler_params=_params("parallel", "parallel"),
        name="diff_attn",
    )(lam_params, qkv, qkv, qkv, subln_g)


MASKED_SCORE = -1e30


def _moba_penalties(gate, blk):
    n_blocks, seq = gate.shape
    block = lax.broadcasted_iota(jnp.int32, (n_blocks, seq), 0)
    query_block = lax.broadcasted_iota(jnp.int32, (n_blocks, seq), 1) // blk
    is_past = block < query_block
    pen = jnp.zeros((n_blocks, seq), F32)
    for n in range(n_blocks - 1):
        g_n = gate[n:n + 1, :]
        ahead = jnp.where(block < n, jnp.where(gate >= g_n, 1.0, 0.0),
                          jnp.where(gate > g_n, 1.0, 0.0))
        rank = jnp.sum(jnp.where(is_past, ahead, 0.0), axis=0, keepdims=True)
        pen_n = jnp.where(rank < C_TOPK, 0.0, MASKED_SCORE)
        pen = jnp.where((block == n) & is_past, pen_n, pen)
    return pen


def _moba_kernel(q_ref, k_ref, v_ref, o_ref, kaug_ref, qaug_ref):
    blk = C_BLOCK
    seq = q_ref.shape[0]
    n_blocks = seq // blk
    causal = _causal_mask(blk)
    low = _low_lanes()

    k = k_ref[...]
    q = q_ref[...]
    kbar = jnp.mean(k.astype(F32).reshape(n_blocks, blk, LANES), axis=1).astype(BF16)
    key_block = lax.broadcasted_iota(jnp.int32, (seq, LANES), 0) // blk
    lane_full = lax.broadcasted_iota(jnp.int32, (seq, LANES), 1)
    blk_row = lax.broadcasted_iota(jnp.int32, (n_blocks, LANES), 0)
    blk_lane = lax.broadcasted_iota(jnp.int32, (n_blocks, LANES), 1)
    zero = jnp.zeros((), BF16)
    for hd in range(2):
        own = low if hd == 0 else jnp.logical_not(low)
        lane0 = LANES // 2 if hd == 0 else 0
        onehot = jnp.where(lane_full - lane0 == key_block, 1.0, 0.0).astype(BF16)
        kaug_ref[hd] = jnp.where(own, k, onehot)
        q_m = jnp.where(own, q, zero)
        pen = _moba_penalties(_dot_nt(kbar, q_m), blk).astype(BF16)
        place = jnp.where(blk_lane - lane0 == blk_row, 1.0, 0.0).astype(BF16)
        placed = lax.dot_general(pen, place, (((0,), (0,)), ((), ())),
                                 preferred_element_type=F32)
        qaug_ref[hd] = q_m + placed.astype(BF16)

    def scores(item):
        i, hd = item
        return _tile_scores(qaug_ref[hd, i * blk:(i + 1) * blk, :], kaug_ref.at[hd], i, blk, causal)

    def output(item, scored):
        return _tile_output(*scored, v_ref, item[0], blk)

    items = [(i, hd) for i in reversed(range(n_blocks)) for hd in range(2)]
    outs = _pipelined(items, scores, output)
    for i in range(n_blocks):
        o_ref[i * blk:(i + 1) * blk, :] = jnp.where(low, outs[(i, 0)], outs[(i, 1)]).astype(BF16)


def _moba(qkv):
    bsz, seq, _ = qkv.shape
    pairs = C_HEADS * C_DH // LANES
    head = lambda off: pl.BlockSpec((None, seq, LANES), lambda b, h: (b, 0, off + h))
    return pl.pallas_call(
        _moba_kernel,
        grid=(bsz, pairs),
        in_specs=[head(0), head(pairs), head(2 * pairs)],
        out_specs=head(0),
        out_shape=jax.ShapeDtypeStruct((bsz, seq, C_HEADS * C_DH), BF16),
        scratch_shapes=[pltpu.VMEM((2, seq, LANES), BF16), pltpu.VMEM((2, seq, LANES), BF16)],
        compiler_params=_params("parallel", "parallel"),
        name="moba_attn",
    )(qkv, qkv, qkv)


def _mem_kv_kernel(mem_ref, wk_ref, wv_ref, k_ref, v_ref):
    mb = mem_ref[...].astype(BF16)
    k_ref[...] = _dot(mb, wk_ref[...]).astype(BF16)
    v_ref[...] = _dot(mb, wv_ref[...]).astype(BF16)


def _mem_kv(mem2d, wk, wv):
    rows = mem2d.shape[0]
    w = pl.BlockSpec((None, D_MODEL, D_MODEL), lambda l, i: (l, 0, 0))
    out = pl.BlockSpec((None, ROW_TILE, D_MODEL), lambda l, i: (l, i, 0))
    shape = jax.ShapeDtypeStruct((DEPTH, rows, D_MODEL), BF16)
    return pl.pallas_call(
        _mem_kv_kernel,
        grid=(DEPTH, rows // ROW_TILE),
        in_specs=[pl.BlockSpec((ROW_TILE, D_MODEL), lambda l, i: (i, 0)), w, w],
        out_specs=[out, out],
        out_shape=[shape, shape],
        compiler_params=_params("parallel", "parallel"),
        name="mem_kv_proj",
    )(mem2d, wk, wv)


def _mixout_xattn_kernel(x_ref, oa_ref, ob_ref, oc_ref, wout_ref, g2_ref, b2_ref,
                         k_ref, v_ref, wq_ref, wo_ref, g_ref, b_ref, o_ref, cat_ref):
    na = A_HEADS * A_DV
    mix = (_dot(oa_ref[...], wout_ref[:na, :])
           + _dot(ob_ref[...], wout_ref[na:na + B_WIDTH, :])
           + _dot(oc_ref[...], wout_ref[na + B_WIDTH:, :]))
    x = _layer_norm(DEEPNORM_ALPHA * x_ref[...] + mix, g2_ref[...], b2_ref[...])
    q = (_dot(x.astype(BF16), wq_ref[...]) * (XA_DH ** -0.5 * LOG2_E)).astype(BF16)
    for h in range(XA_HEADS):
        sl = slice(h * XA_DH, (h + 1) * XA_DH)
        s = _dot_nt(q[:, sl], k_ref[:, sl])
        p = jnp.exp2(s - jnp.max(s, axis=-1, keepdims=True))
        denom = jnp.sum(p, axis=-1, keepdims=True)
        cat_ref[:, sl] = (_dot(p.astype(BF16), v_ref[:, sl]) / denom).astype(BF16)
    y = _dot(cat_ref[...], wo_ref[...])
    o_ref[...] = _layer_norm(DEEPNORM_ALPHA * x + y, g_ref[...], b_ref[...])


def _mixout_xattn(x3, oa, ob, oc, w_out, g2, b2, k, v, wq, wo, g3, b3):
    bsz, seq, _ = x3.shape
    rows = lambda n: pl.BlockSpec((None, ROW_TILE, n), lambda b_, i: (b_, i, 0))
    kv = pl.BlockSpec((None, MEM_LEN, D_MODEL), lambda b_, i: (b_, 0, 0))
    weight = _resident((D_MODEL, D_MODEL))
    vec = _resident((1, D_MODEL))
    return pl.pallas_call(
        _mixout_xattn_kernel,
        grid=(bsz, seq // ROW_TILE),
        in_specs=[rows(D_MODEL), rows(A_HEADS * A_DV), rows(B_WIDTH), rows(C_HEADS * C_DH),
                  weight, vec, vec, kv, kv, weight, weight, vec, vec],
        out_specs=rows(D_MODEL),
        out_shape=jax.ShapeDtypeStruct((bsz, seq, D_MODEL), F32),
        scratch_shapes=[pltpu.VMEM((ROW_TILE, D_MODEL), BF16)],
        compiler_params=_params("parallel", "parallel"),
        name="mixout_xattn_ln",
    )(x3, oa, ob, oc, w_out, g2, b2, k, v, wq, wo, g3, b3)


def kernel(x, mem, positions, ffn1_w_gate, ffn1_w_up, ffn1_w_down, ln1_g, ln1_b, mix_w_in, diff_lq1, diff_lk1, diff_lq2, diff_lk2, diff_subln_g, sgu_ln_g, sgu_ln_b, sgu_w, sgu_b, mix_w_out, ln2_g, ln2_b, xa_wq, xa_wk, xa_wv, xa_wo, ln3_g, ln3_b, ffn2_w_gate, ffn2_w_up, ffn2_w_down, ln4_g, ln4_b):
    bsz, seq, d = x.shape
    t = bsz * seq
    bf = lambda w: w.astype(BF16)
    row = lambda p, l: p[l].reshape(1, -1)

    cos_t, sin_t = _rope_tables(positions)
    mem_k, mem_v = _mem_kv(mem.reshape(bsz * MEM_LEN, d), bf(xa_wk), bf(xa_wv))
    sgu_bias = jnp.repeat(jnp.swapaxes(sgu_b, 1, 2), B_DG, axis=2)
    lam_params = jnp.stack([diff_lq1, diff_lk1, diff_lq2, diff_lk2], axis=1)

    h = x.reshape(t, d)
    for l in range(DEPTH):
        lam_init = 0.8 - 0.6 * math.exp(-0.3 * l)
        h = _ffn_ln(h, bf(ffn1_w_gate[l]), bf(ffn1_w_up[l]), bf(ffn1_w_down[l]),
                    row(ln1_g, l), row(ln1_b, l))
        qkv_a, ob, qkv_c = _inproj(h, bf(mix_w_in[l]), cos_t, sin_t, row(sgu_ln_g, l),
                                   row(sgu_ln_b, l), sgu_w[l], sgu_bias[l])
        oa = _diff_attn(qkv_a.reshape(bsz, seq, A_COLS), lam_params[l], row(diff_subln_g, l), lam_init)
        oc = _moba(qkv_c.reshape(bsz, seq, C_COLS))
        h = _mixout_xattn(h.reshape(bsz, seq, d), oa, ob.reshape(bsz, seq, B_WIDTH), oc,
                          bf(mix_w_out[l]), row(ln2_g, l), row(ln2_b, l),
                          mem_k[l].reshape(bsz, MEM_LEN, d), mem_v[l].reshape(bsz, MEM_LEN, d),
                          bf(xa_wq[l]), bf(xa_wo[l]), row(ln3_g, l), row(ln3_b, l)).reshape(t, d)
        h = _ffn_ln(h, bf(ffn2_w_gate[l]), bf(ffn2_w_up[l]), bf(ffn2_w_down[l]),
                    row(ln4_g, l), row(ln4_b, l))
    return h.reshape(bsz, seq, d)
```

```python
import functools
import math

import jax
import jax.numpy as jnp
from jax import lax
from jax.experimental import pallas as pl
from jax.experimental.pallas import tpu as pltpu

D_MODEL = 1024
DEPTH = 2
MEM_LEN = 256
A_HEADS = 4
A_DH = 64
A_DV = 2 * A_DH
B_GROUPS = 4
B_DG = 64
B_WIDTH = B_GROUPS * B_DG
B_CHUNK = 128
C_HEADS = 4
C_DH = 64
C_BLOCK = 256
C_TOPK = 3
A_COLS = 3 * A_HEADS * A_DV
C_COLS = 3 * C_HEADS * C_DH
IN_COLS = A_COLS + 2 * B_WIDTH + C_COLS
XA_HEADS = 4
XA_DH = D_MODEL // XA_HEADS
D_FF = 2816
ROPE_THETA = 10000.0
LN_EPS = 1e-5
DEEPNORM_ALPHA = (2.0 * DEPTH) ** 0.25

LANES = 128
MXU_WIDTH = 256
VMEM_LIMIT_BYTES = 56 * 1024 * 1024

ROW_TILE = 1024
SUB_TILE = 512
FF_CHUNK = MXU_WIDTH
ATTN_TILE = 256

F32 = jnp.float32
BF16 = jnp.bfloat16
NEG_INF = float("-inf")
LOG2_E = math.log2(math.e)


def _params(*sem):
    return pltpu.CompilerParams(dimension_semantics=sem, vmem_limit_bytes=VMEM_LIMIT_BYTES)


def _resident(shape, layer):
    index = (layer,) + (0,) * len(shape)
    return pl.BlockSpec((None,) + tuple(shape), lambda *_: index, pipeline_mode=pl.Buffered(1))


def _layer_norm(y, g, b):
    mu = jnp.mean(y, axis=-1, keepdims=True)
    d = y - mu
    var = jnp.mean(d * d, axis=-1, keepdims=True)
    return d * lax.rsqrt(var + LN_EPS) * g + b


def _interleave_sub_tiles(stages):
    gens = [stages(slice(s * SUB_TILE, (s + 1) * SUB_TILE)) for s in range(ROW_TILE // SUB_TILE)]
    live = list(range(len(gens)))
    step = 0
    while live:
        for s in list(live):
            if step >= s:
                try:
                    next(gens[s])
                except StopIteration:
                    live.remove(s)
        step += 1


def _dot(a, b):
    return jnp.dot(a, b, preferred_element_type=F32)


def _dot_nt(a, b):
    return lax.dot_general(a, b, (((1,), (1,)), ((), ())), preferred_element_type=F32)


def _rope_kernel(pos_ref, invf_ref, cos_ref, sin_ref):
    ang = pos_ref[...] * invf_ref[...]
    lane = lax.broadcasted_iota(jnp.int32, (1, LANES), 1)
    sign = jnp.where(lane % A_DH < A_DH // 2, -1.0, 1.0).astype(F32)
    cos_ref[...] = jnp.cos(ang)
    sin_ref[...] = jnp.sin(ang) * sign


def _rope_tables(positions):
    t = positions.size
    half = A_DH // 2
    inv_freq = 1.0 / (ROPE_THETA ** (jnp.arange(0, A_DH, 2, dtype=F32) / A_DH))
    invf = jnp.tile(inv_freq, LANES // half).reshape(1, LANES)
    pos = positions.astype(F32).reshape(t, 1)
    tm = 2048
    out = jax.ShapeDtypeStruct((t, LANES), F32)
    return pl.pallas_call(
        _rope_kernel,
        grid=(t // tm,),
        in_specs=[pl.BlockSpec((tm, 1), lambda i: (i, 0)),
                  pl.BlockSpec((1, LANES), lambda i: (0, 0))],
        out_specs=[pl.BlockSpec((tm, LANES), lambda i: (i, 0))] * 2,
        out_shape=[out, out],
        compiler_params=_params("parallel"),
        name="rope_tables",
    )(pos, invf)


def _ffn_ln_kernel(x_ref, wg_ref, wu_ref, wd_ref, g_ref, b_ref, o_ref, acc_ref):
    def sub_tile(rows):
        xb = x_ref[rows, :].astype(BF16)
        for c in range(D_FF // FF_CHUNK):
            cols = slice(c * FF_CHUNK, (c + 1) * FF_CHUNK)
            gate = _dot(xb, wg_ref[:, cols])
            up = _dot(xb, wu_ref[:, cols])
            act = (gate * jax.nn.sigmoid(gate) * up).astype(BF16)
            part = _dot(act, wd_ref[cols, :])
            if c == 0:
                acc_ref[rows, :] = part
            else:
                acc_ref[rows, :] += part
            yield
        y = DEEPNORM_ALPHA * x_ref[rows, :] + 0.5 * acc_ref[rows, :]
        o_ref[rows, :] = _layer_norm(y, g_ref[...], b_ref[...])

    _interleave_sub_tiles(sub_tile)


def _ffn_ln(x, wg, wu, wd, g, b, layer):
    t = x.shape[0]
    row = pl.BlockSpec((ROW_TILE, D_MODEL), lambda i: (i, 0))
    return pl.pallas_call(
        _ffn_ln_kernel,
        grid=(t // ROW_TILE,),
        in_specs=[row, _resident((D_MODEL, D_FF), layer), _resident((D_MODEL, D_FF), layer),
                  _resident((D_FF, D_MODEL), layer), _resident((1, D_MODEL), layer),
                  _resident((1, D_MODEL), layer)],
        out_specs=row,
        out_shape=jax.ShapeDtypeStruct((t, D_MODEL), F32),
        scratch_shapes=[pltpu.VMEM((ROW_TILE, D_MODEL), F32)],
        compiler_params=_params("parallel"),
        name="ffn_ln",
    )(x, wg, wu, wd, g, b)


def _gelu(y):
    return 0.5 * y * (1.0 + lax.erf(y * (2.0 ** -0.5)))


def _inproj_kernel(x_ref, w_ref, cos_ref, sin_ref, lng_ref, lnb_ref, sw_ref, sb_ref,
                   a_ref, ob_ref, c_ref):
    lane = lax.broadcasted_iota(jnp.int32, (1, LANES), 1)
    first_half = lane % A_DH < A_DH // 2
    low_lanes = lane < B_DG
    tri_r = lax.broadcasted_iota(jnp.int32, (B_CHUNK, B_CHUNK), 0)
    tri_c = lax.broadcasted_iota(jnp.int32, (B_CHUNK, B_CHUNK), 1)
    w_tril = [jnp.where(tri_r >= tri_c, sw_ref[g], 0.0).astype(BF16) for g in range(B_GROUPS)]
    zero = jnp.zeros((), BF16)
    q_scale = A_DH ** -0.5 * LOG2_E

    def halves(y):
        return y[:, :LANES], y[:, LANES:]

    def sub_tile(rows):
        xb = x_ref[rows, :].astype(BF16)
        cos = cos_ref[rows, :]
        sin = sin_ref[rows, :]

        def rope(y, scale):
            swapped = jnp.where(first_half, pltpu.roll(y, LANES - A_DH // 2, 1),
                                pltpu.roll(y, A_DH // 2, 1))
            return ((y * cos + swapped * sin) * scale).astype(BF16)

        def project(col):
            return _dot(xb, w_ref[:, col:col + MXU_WIDTH])

        for j in range(2 * A_HEADS * A_DV // MXU_WIDTH):
            col = j * MXU_WIDTH
            scale = q_scale if col < A_HEADS * A_DV else 1.0
            lo, hi = halves(project(col))
            a_ref[rows, col:col + LANES] = rope(lo, scale)
            a_ref[rows, col + LANES:col + MXU_WIDTH] = rope(hi, scale)
            yield
        for j in range(A_HEADS * A_DV // MXU_WIDTH):
            col = 2 * A_HEADS * A_DV + j * MXU_WIDTH
            a_ref[rows, col:col + MXU_WIDTH] = project(col).astype(BF16)
            yield

        c0 = A_COLS + 2 * B_WIDTH
        for j in range(2):
            scale = C_DH ** -0.5 * LOG2_E if j == 0 else 1.0
            lo, hi = halves(project(c0 + j * MXU_WIDTH))
            c_ref[rows, j * MXU_WIDTH:j * MXU_WIDTH + LANES] = rope(lo, scale)
            c_ref[rows, j * MXU_WIDTH + LANES:(j + 1) * MXU_WIDTH] = rope(hi, scale)
            yield
        c_ref[rows, 2 * MXU_WIDTH:] = project(c0 + 2 * MXU_WIDTH).astype(BF16)
        yield

        u = _gelu(project(A_COLS))
        yield
        v = _gelu(project(A_COLS + B_WIDTH))
        vn = _layer_norm(v, lng_ref[...], lnb_ref[...]).astype(BF16)
        yield
        for ci in range(SUB_TILE // B_CHUNK):
            rsl = slice(ci * B_CHUNK, (ci + 1) * B_CHUNK)
            out_rows = slice(rows.start + ci * B_CHUNK, rows.start + (ci + 1) * B_CHUNK)
            for hb in range(B_WIDTH // LANES):
                csl = slice(hb * LANES, (hb + 1) * LANES)
                vblk = vn[rsl, csl]
                mix = (_dot(w_tril[2 * hb], jnp.where(low_lanes, vblk, zero))
                       + _dot(w_tril[2 * hb + 1], jnp.where(low_lanes, zero, vblk))
                       + sb_ref[:, csl])
                ob_ref[out_rows, csl] = (u[rsl, csl] * mix).astype(BF16)

    _interleave_sub_tiles(sub_tile)


def _inproj(x, w_in, cos_t, sin_t, ln_g, ln_b, sgu_w, sgu_bias, layer):
    t = x.shape[0]
    rows = lambda n: pl.BlockSpec((ROW_TILE, n), lambda i: (i, 0))
    return pl.pallas_call(
        _inproj_kernel,
        grid=(t // ROW_TILE,),
        in_specs=[rows(D_MODEL), _resident((D_MODEL, IN_COLS), layer), rows(LANES), rows(LANES),
                  _resident((1, B_WIDTH), layer), _resident((1, B_WIDTH), layer),
                  _resident((B_GROUPS, B_CHUNK, B_CHUNK), layer),
                  _resident((B_CHUNK, B_WIDTH), layer)],
        out_specs=[rows(A_COLS), rows(B_WIDTH), rows(C_COLS)],
        out_shape=[jax.ShapeDtypeStruct((t, A_COLS), BF16),
                   jax.ShapeDtypeStruct((t, B_WIDTH), BF16),
                   jax.ShapeDtypeStruct((t, C_COLS), BF16)],
        compiler_params=_params("parallel"),
        name="mixer_inproj",
    )(x, w_in, cos_t, sin_t, ln_g, ln_b, sgu_w, sgu_bias)


def _tile_scores(q_m, k_ref, i, tile, causal):
    d0 = i * tile
    s = _dot_nt(q_m, k_ref[0:d0 + tile, :])
    s_d = jnp.where(causal, s[:, d0:], NEG_INF)
    m = jnp.max(s_d, axis=-1, keepdims=True)
    if i == 0:
        return [s_d], m
    s_p = s[:, :d0]
    return [s_p, s_d], jnp.maximum(m, jnp.max(s_p, axis=-1, keepdims=True))


def _tile_output(parts, m, v_ref, i, tile):
    p = [jnp.exp2(s - m) for s in parts]
    denom = jnp.sum(p[0], axis=-1, keepdims=True)
    for extra in p[1:]:
        denom = denom + jnp.sum(extra, axis=-1, keepdims=True)
    p = [x.astype(BF16) for x in p]
    p = p[0] if len(p) == 1 else jnp.concatenate(p, axis=1)
    return _dot(p, v_ref[0:(i + 1) * tile, :]) / denom


PIPELINE_AHEAD = 2


def _pipelined(items, score_stage, output_stage):
    outs = {}
    scored = {}
    for j in range(len(items) + PIPELINE_AHEAD):
        if j < len(items):
            scored[items[j]] = score_stage(items[j])
        if j >= PIPELINE_AHEAD:
            item = items[j - PIPELINE_AHEAD]
            outs[item] = output_stage(item, scored.pop(item))
    return outs


def _causal_mask(tile):
    row = lax.broadcasted_iota(jnp.int32, (tile, tile), 0)
    col = lax.broadcasted_iota(jnp.int32, (tile, tile), 1)
    return col <= row


def _low_lanes():
    return lax.broadcasted_iota(jnp.int32, (1, LANES), 1) < LANES // 2


def _map_query(q_ref, i, tile, mp):
    q = q_ref[i * tile:(i + 1) * tile, :]
    keep = _low_lanes() if mp == 0 else jnp.logical_not(_low_lanes())
    return jnp.where(keep, q, jnp.zeros((), q.dtype))


def _diff_attn_kernel(lam_ref, q_ref, k_ref, v_ref, g_ref, o_ref, *, lam_init):
    tile = ATTN_TILE
    n_tiles = q_ref.shape[0] // tile
    causal = _causal_mask(tile)
    lp = lam_ref[...]
    lam = (jnp.exp(jnp.sum(lp[0:1] * lp[1:2], axis=-1, keepdims=True))
           - jnp.exp(jnp.sum(lp[2:3] * lp[3:4], axis=-1, keepdims=True)) + lam_init)
    gain = g_ref[...] * (1.0 - lam_init)

    def scores(item):
        i, mp = item
        return _tile_scores(_map_query(q_ref, i, tile, mp), k_ref, i, tile, causal)

    def output(item, scored):
        return _tile_output(*scored, v_ref, item[0], tile)

    items = [(i, mp) for i in reversed(range(n_tiles)) for mp in range(2)]
    outs = _pipelined(items, scores, output)
    for i in range(n_tiles):
        o = outs[(i, 0)] - lam * outs[(i, 1)]
        ms = jnp.mean(o * o, axis=-1, keepdims=True)
        o_ref[i * tile:(i + 1) * tile, :] = (o * lax.rsqrt(ms + LN_EPS) * gain).astype(BF16)


def _diff_attn(qkv, lam_params, subln_g, lam_init, layer):
    bsz, seq, _ = qkv.shape
    head = lambda off: pl.BlockSpec((None, seq, LANES), lambda b, h: (b, 0, off + h))
    return pl.pallas_call(
        functools.partial(_diff_attn_kernel, lam_init=lam_init),
        grid=(bsz, A_HEADS),
        in_specs=[_resident((4, A_DH), layer), head(0), head(A_HEADS), head(2 * A_HEADS),
                  _resident((1, A_DV), layer)],
        out_specs=head(0),
        out_shape=jax.ShapeDtypeStruct((bsz, seq, A_HEADS * A_DV), BF16),
        compiler_params=_params("parallel", "parallel"),
        name="diff_attn",
    )(lam_params, qkv, qkv, qkv, subln_g)


MASKED_SCORE = -1e30


def _moba_penalties(gate, blk):
    n_blocks, seq = gate.shape
    block = lax.broadcasted_iota(jnp.int32, (n_blocks, seq), 0)
    query_block = lax.broadcasted_iota(jnp.int32, (n_blocks, seq), 1) // blk
    is_past = block < query_block
    pen = jnp.zeros((n_blocks, seq), F32)
    for n in range(n_blocks - 1):
        g_n = gate[n:n + 1, :]
        ahead = jnp.where(block < n, jnp.where(gate >= g_n, 1.0, 0.0),
                          jnp.where(gate > g_n, 1.0, 0.0))
        rank = jnp.sum(jnp.where(is_past, ahead, 0.0), axis=0, keepdims=True)
        pen_n = jnp.where(rank < C_TOPK, 0.0, MASKED_SCORE)
        pen = jnp.where((block == n) & is_past, pen_n, pen)
    return pen


def _moba_kernel(q_ref, k_ref, v_ref, o_ref, kaug_ref, qaug_ref):
    blk = C_BLOCK
    seq = q_ref.shape[0]
    n_blocks = seq // blk
    causal = _causal_mask(blk)
    low = _low_lanes()

    k = k_ref[...]
    q = q_ref[...]
    kbar = jnp.mean(k.astype(F32).reshape(n_blocks, blk, LANES), axis=1).astype(BF16)
    key_block = lax.broadcasted_iota(jnp.int32, (seq, LANES), 0) // blk
    lane_full = lax.broadcasted_iota(jnp.int32, (seq, LANES), 1)
    blk_row = lax.broadcasted_iota(jnp.int32, (n_blocks, LANES), 0)
    blk_lane = lax.broadcasted_iota(jnp.int32, (n_blocks, LANES), 1)
    zero = jnp.zeros((), BF16)
    for hd in range(2):
        own = low if hd == 0 else jnp.logical_not(low)
        lane0 = LANES // 2 if hd == 0 else 0
        onehot = jnp.where(lane_full - lane0 == key_block, 1.0, 0.0).astype(BF16)
        kaug_ref[hd] = jnp.where(own, k, onehot)
        q_m = jnp.where(own, q, zero)
        pen = _moba_penalties(_dot_nt(kbar, q_m), blk).astype(BF16)
        place = jnp.where(blk_lane - lane0 == blk_row, 1.0, 0.0).astype(BF16)
        placed = lax.dot_general(pen, place, (((0,), (0,)), ((), ())),
                                 preferred_element_type=F32)
        qaug_ref[hd] = q_m + placed.astype(BF16)

    def scores(item):
        i, hd = item
        return _tile_scores(qaug_ref[hd, i * blk:(i + 1) * blk, :], kaug_ref.at[hd], i, blk, causal)

    def output(item, scored):
        return _tile_output(*scored, v_ref, item[0], blk)

    items = [(i, hd) for i in reversed(range(n_blocks)) for hd in range(2)]
    outs = _pipelined(items, scores, output)
    for i in range(n_blocks):
        o_ref[i * blk:(i + 1) * blk, :] = jnp.where(low, outs[(i, 0)], outs[(i, 1)]).astype(BF16)


def _moba(qkv):
    bsz, seq, _ = qkv.shape
    pairs = C_HEADS * C_DH // LANES
    head = lambda off: pl.BlockSpec((None, seq, LANES), lambda b, h: (b, 0, off + h))
    return pl.pallas_call(
        _moba_kernel,
        grid=(bsz, pairs),
        in_specs=[head(0), head(pairs), head(2 * pairs)],
        out_specs=head(0),
        out_shape=jax.ShapeDtypeStruct((bsz, seq, C_HEADS * C_DH), BF16),
        scratch_shapes=[pltpu.VMEM((2, seq, LANES), BF16), pltpu.VMEM((2, seq, LANES), BF16)],
        compiler_params=_params("parallel", "parallel"),
        name="moba_attn",
    )(qkv, qkv, qkv)


def _mem_kv_kernel(mem_ref, wk_ref, wv_ref, k_ref, v_ref):
    mb = mem_ref[...].astype(BF16)
    k_ref[...] = _dot(mb, wk_ref[...]).astype(BF16)
    v_ref[...] = _dot(mb, wv_ref[...]).astype(BF16)


def _mem_kv(mem2d, wk, wv):
    rows = mem2d.shape[0]
    w = pl.BlockSpec((None, D_MODEL, D_MODEL), lambda l, i: (l, 0, 0))
    out = pl.BlockSpec((None, ROW_TILE, D_MODEL), lambda l, i: (l, i, 0))
    shape = jax.ShapeDtypeStruct((DEPTH, rows, D_MODEL), BF16)
    return pl.pallas_call(
        _mem_kv_kernel,
        grid=(DEPTH, rows // ROW_TILE),
        in_specs=[pl.BlockSpec((ROW_TILE, D_MODEL), lambda l, i: (i, 0)), w, w],
        out_specs=[out, out],
        out_shape=[shape, shape],
        compiler_params=_params("parallel", "parallel"),
        name="mem_kv_proj",
    )(mem2d, wk, wv)


def _mixout_xattn_kernel(x_ref, oa_ref, ob_ref, oc_ref, wout_ref, g2_ref, b2_ref,
                         k_ref, v_ref, wq_ref, wo_ref, g_ref, b_ref, o_ref, cat_ref):
    na = A_HEADS * A_DV

    def sub_tile(rows):
        mix = (_dot(oa_ref[rows, :], wout_ref[:na, :])
               + _dot(ob_ref[rows, :], wout_ref[na:na + B_WIDTH, :])
               + _dot(oc_ref[rows, :], wout_ref[na + B_WIDTH:, :]))
        yield
        x = _layer_norm(DEEPNORM_ALPHA * x_ref[rows, :] + mix, g2_ref[...], b2_ref[...])
        q = (_dot(x.astype(BF16), wq_ref[...]) * (XA_DH ** -0.5 * LOG2_E)).astype(BF16)
        yield
        for h in range(XA_HEADS):
            sl = slice(h * XA_DH, (h + 1) * XA_DH)
            s = _dot_nt(q[:, sl], k_ref[:, sl])
            p = jnp.exp2(s - jnp.max(s, axis=-1, keepdims=True))
            denom = jnp.sum(p, axis=-1, keepdims=True)
            cat_ref[rows, sl] = (_dot(p.astype(BF16), v_ref[:, sl]) / denom).astype(BF16)
            yield
        y = _dot(cat_ref[rows, :], wo_ref[...])
        yield
        o_ref[rows, :] = _layer_norm(DEEPNORM_ALPHA * x + y, g_ref[...], b_ref[...])

    _interleave_sub_tiles(sub_tile)


def _mixout_xattn(x3, oa, ob, oc, w_out, g2, b2, k, v, wq, wo, g3, b3, layer):
    bsz, seq, _ = x3.shape
    rows = lambda n: pl.BlockSpec((None, ROW_TILE, n), lambda b_, i: (b_, i, 0))
    kv = pl.BlockSpec((None, None, MEM_LEN, D_MODEL), lambda b_, i: (layer, b_, 0, 0))
    weight = _resident((D_MODEL, D_MODEL), layer)
    vec = _resident((1, D_MODEL), layer)
    return pl.pallas_call(
        _mixout_xattn_kernel,
        grid=(bsz, seq // ROW_TILE),
        in_specs=[rows(D_MODEL), rows(A_HEADS * A_DV), rows(B_WIDTH), rows(C_HEADS * C_DH),
                  weight, vec, vec, kv, kv, weight, weight, vec, vec],
        out_specs=rows(D_MODEL),
        out_shape=jax.ShapeDtypeStruct((bsz, seq, D_MODEL), F32),
        scratch_shapes=[pltpu.VMEM((ROW_TILE, D_MODEL), BF16)],
        compiler_params=_params("parallel", "parallel"),
        name="mixout_xattn_ln",
    )(x3, oa, ob, oc, w_out, g2, b2, k, v, wq, wo, g3, b3)


def kernel(x, mem, positions, ffn1_w_gate, ffn1_w_up, ffn1_w_down, ln1_g, ln1_b, mix_w_in, diff_lq1, diff_lk1, diff_lq2, diff_lk2, diff_subln_g, sgu_ln_g, sgu_ln_b, sgu_w, sgu_b, mix_w_out, ln2_g, ln2_b, xa_wq, xa_wk, xa_wv, xa_wo, ln3_g, ln3_b, ffn2_w_gate, ffn2_w_up, ffn2_w_down, ln4_g, ln4_b):
    bsz, seq, d = x.shape
    t = bsz * seq
    bf = lambda w: w.astype(BF16)
    vec = lambda p: p.reshape(DEPTH, 1, -1)

    cos_t, sin_t = _rope_tables(positions)
    mem_k, mem_v = _mem_kv(mem.reshape(bsz * MEM_LEN, d), bf(xa_wk), bf(xa_wv))
    mem_k = mem_k.reshape(DEPTH, bsz, MEM_LEN, d)
    mem_v = mem_v.reshape(DEPTH, bsz, MEM_LEN, d)
    sgu_bias = jnp.repeat(jnp.swapaxes(sgu_b, 1, 2), B_DG, axis=2)
    lam_params = jnp.stack([diff_lq1, diff_lk1, diff_lq2, diff_lk2], axis=1)
    ffn1 = (bf(ffn1_w_gate), bf(ffn1_w_up), bf(ffn1_w_down), vec(ln1_g), vec(ln1_b))
    ffn2 = (bf(ffn2_w_gate), bf(ffn2_w_up), bf(ffn2_w_down), vec(ln4_g), vec(ln4_b))
    w_in, w_out, wq, wo = bf(mix_w_in), bf(mix_w_out), bf(xa_wq), bf(xa_wo)

    h = x.reshape(t, d)
    for l in range(DEPTH):
        lam_init = 0.8 - 0.6 * math.exp(-0.3 * l)
        h = _ffn_ln(h, *ffn1, l)
        qkv_a, ob, qkv_c = _inproj(h, w_in, cos_t, sin_t, vec(sgu_ln_g), vec(sgu_ln_b),
                                   sgu_w, sgu_bias, l)
        oa = _diff_attn(qkv_a.reshape(bsz, seq, A_COLS), lam_params, vec(diff_subln_g), lam_init, l)
        oc = _moba(qkv_c.reshape(bsz, seq, C_COLS))
        h = _mixout_xattn(h.reshape(bsz, seq, d), oa, ob.reshape(bsz, seq, B_WIDTH), oc,
                          w_out, vec(ln2_g), vec(ln2_b), mem_k, mem_v, wq, wo,
                          vec(ln3_g), vec(ln3_b), l).reshape(t, d)
        h = _ffn_ln(h, *ffn2, l)
    return h.reshape(bsz, seq, d)
```

```python
import functools
import math

import jax
import jax.numpy as jnp
from jax import lax
from jax.experimental import pallas as pl
from jax.experimental.pallas import tpu as pltpu

D_MODEL = 1024
DEPTH = 2
MEM_LEN = 256
A_HEADS = 4
A_DH = 64
A_DV = 2 * A_DH
B_GROUPS = 4
B_DG = 64
B_WIDTH = B_GROUPS * B_DG
B_CHUNK = 128
C_HEADS = 4
C_DH = 64
C_BLOCK = 256
C_TOPK = 3
A_COLS = 3 * A_HEADS * A_DV
C_COLS = 3 * C_HEADS * C_DH
IN_COLS = A_COLS + 2 * B_WIDTH + C_COLS
XA_HEADS = 4
XA_DH = D_MODEL // XA_HEADS
D_FF = 2816
ROPE_THETA = 10000.0
LN_EPS = 1e-5
DEEPNORM_ALPHA = (2.0 * DEPTH) ** 0.25

LANES = 128
MXU_WIDTH = 256
VMEM_LIMIT_BYTES = 56 * 1024 * 1024

ROW_TILE = 1024
SUB_TILE = 512
INPROJ_SUB_TILE = 256
FF_CHUNK = MXU_WIDTH
ATTN_TILE = 256

F32 = jnp.float32
BF16 = jnp.bfloat16
NEG_INF = float("-inf")
LOG2_E = math.log2(math.e)


def _params(*sem):
    return pltpu.CompilerParams(dimension_semantics=sem, vmem_limit_bytes=VMEM_LIMIT_BYTES)


def _resident(shape, layer):
    index = (layer,) + (0,) * len(shape)
    return pl.BlockSpec((None,) + tuple(shape), lambda *_: index, pipeline_mode=pl.Buffered(1))


def _layer_norm(y, g, b, eps=LN_EPS):
    mu = jnp.mean(y, axis=-1, keepdims=True)
    d = y - mu
    var = jnp.mean(d * d, axis=-1, keepdims=True)
    return d * lax.rsqrt(var + eps) * g + b


def _interleave_sub_tiles(stages, sub_tile=None):
    sub_tile = sub_tile or SUB_TILE
    gens = [stages(slice(s * sub_tile, (s + 1) * sub_tile)) for s in range(ROW_TILE // sub_tile)]
    live = list(range(len(gens)))
    step = 0
    while live:
        for s in list(live):
            if step >= s:
                try:
                    next(gens[s])
                except StopIteration:
                    live.remove(s)
        step += 1


def _dot(a, b):
    return jnp.dot(a, b, preferred_element_type=F32)


def _dot_nt(a, b):
    return lax.dot_general(a, b, (((1,), (1,)), ((), ())), preferred_element_type=F32)


ROPE_PACK = LANES // (A_DH // 2)


def _rope_kernel(pos_ref, invf_ref, cos_ref, sin_ref):
    half = A_DH // 2
    ang = pos_ref[...] * invf_ref[...]
    lane = lax.broadcasted_iota(jnp.int32, (1, LANES), 1)
    group = lane // half
    sign = jnp.where(lane % A_DH < half, -1.0, 1.0).astype(F32)
    packed_rows = pos_ref.shape[0]
    for table, out_ref, scale in ((jnp.cos(ang), cos_ref, None), (jnp.sin(ang), sin_ref, sign)):
        rolled = [table] + [pltpu.roll(table, half * k, 1) for k in range(1, ROPE_PACK)]
        for j in range(ROPE_PACK):
            spread = rolled[(0 - j) % ROPE_PACK]
            for g in range(1, ROPE_PACK):
                spread = jnp.where(group == g, rolled[(g - j) % ROPE_PACK], spread)
            if scale is not None:
                spread = spread * scale
            out_ref[pl.ds(j, packed_rows, stride=ROPE_PACK), :] = spread


def _rope_tables(positions):
    t = positions.size
    half = A_DH // 2
    inv_freq = 1.0 / (ROPE_THETA ** (jnp.arange(0, A_DH, 2, dtype=F32) / A_DH))
    invf = jnp.tile(inv_freq, ROPE_PACK).reshape(1, LANES)
    pos = jnp.repeat(positions.astype(F32).reshape(t // ROPE_PACK, ROPE_PACK), half, axis=1)
    tm = 2048
    out = jax.ShapeDtypeStruct((t, LANES), F32)
    return pl.pallas_call(
        _rope_kernel,
        grid=(t // tm,),
        in_specs=[pl.BlockSpec((tm // ROPE_PACK, LANES), lambda i: (i, 0)),
                  pl.BlockSpec((1, LANES), lambda i: (0, 0))],
        out_specs=[pl.BlockSpec((tm, LANES), lambda i: (i, 0))] * 2,
        out_shape=[out, out],
        compiler_params=_params("parallel"),
        name="rope_tables",
    )(pos, invf)


def _ffn_ln_kernel(x_ref, wg_ref, wu_ref, wd_ref, g_ref, b_ref, o_ref, acc_ref):
    def sub_tile(rows):
        xb = x_ref[rows, :].astype(BF16)
        for c in range(D_FF // FF_CHUNK):
            cols = slice(c * FF_CHUNK, (c + 1) * FF_CHUNK)
            gate = _dot(xb, wg_ref[:, cols])
            up = _dot(xb, wu_ref[:, cols])
            act = (gate * jax.nn.sigmoid(gate) * up).astype(BF16)
            part = _dot(act, wd_ref[cols, :])
            if c == 0:
                acc_ref[rows, :] = (2.0 * DEEPNORM_ALPHA) * x_ref[rows, :] + part
            else:
                acc_ref[rows, :] += part
            yield
        o_ref[rows, :] = _layer_norm(acc_ref[rows, :], g_ref[...], b_ref[...], eps=4.0 * LN_EPS)

    _interleave_sub_tiles(sub_tile)


def _ffn_ln(x, wg, wu, wd, g, b, layer):
    t = x.shape[0]
    row = pl.BlockSpec((ROW_TILE, D_MODEL), lambda i: (i, 0))
    return pl.pallas_call(
        _ffn_ln_kernel,
        grid=(t // ROW_TILE,),
        in_specs=[row, _resident((D_MODEL, D_FF), layer), _resident((D_MODEL, D_FF), layer),
                  _resident((D_FF, D_MODEL), layer), _resident((1, D_MODEL), layer),
                  _resident((1, D_MODEL), layer)],
        out_specs=row,
        out_shape=jax.ShapeDtypeStruct((t, D_MODEL), F32),
        scratch_shapes=[pltpu.VMEM((ROW_TILE, D_MODEL), F32)],
        compiler_params=_params("parallel"),
        name="ffn_ln",
    )(x, wg, wu, wd, g, b)


def _gelu(y):
    return 0.5 * y * (1.0 + lax.erf(y * (2.0 ** -0.5)))


def _inproj_kernel(x_ref, w_ref, cos_ref, sin_ref, lng_ref, lnb_ref, sw_ref, sb_ref,
                   a_ref, ob_ref, c_ref):
    lane = lax.broadcasted_iota(jnp.int32, (1, LANES), 1)
    first_half = lane % A_DH < A_DH // 2
    low_lanes = lane < B_DG
    tri_r = lax.broadcasted_iota(jnp.int32, (B_CHUNK, B_CHUNK), 0)
    tri_c = lax.broadcasted_iota(jnp.int32, (B_CHUNK, B_CHUNK), 1)
    w_tril = [jnp.where(tri_r >= tri_c, sw_ref[g], 0.0).astype(BF16) for g in range(B_GROUPS)]
    zero = jnp.zeros((), BF16)
    q_scale = A_DH ** -0.5 * LOG2_E

    def halves(y):
        return y[:, :LANES], y[:, LANES:]

    def sub_tile(rows):
        xb = x_ref[rows, :].astype(BF16)
        cos = cos_ref[rows, :]
        sin = sin_ref[rows, :]

        def rope(y, scale):
            swapped = jnp.where(first_half, pltpu.roll(y, LANES - A_DH // 2, 1),
                                pltpu.roll(y, A_DH // 2, 1))
            return ((y * cos + swapped * sin) * scale).astype(BF16)

        def project(col):
            return _dot(xb, w_ref[:, col:col + MXU_WIDTH])

        for j in range(2 * A_HEADS * A_DV // MXU_WIDTH):
            col = j * MXU_WIDTH
            scale = q_scale if col < A_HEADS * A_DV else 1.0
            lo, hi = halves(project(col))
            a_ref[rows, col:col + LANES] = rope(lo, scale)
            a_ref[rows, col + LANES:col + MXU_WIDTH] = rope(hi, scale)
            yield
        for j in range(A_HEADS * A_DV // MXU_WIDTH):
            col = 2 * A_HEADS * A_DV + j * MXU_WIDTH
            a_ref[rows, col:col + MXU_WIDTH] = project(col).astype(BF16)
            yield

        c0 = A_COLS + 2 * B_WIDTH
        for j in range(2):
            scale = C_DH ** -0.5 * LOG2_E if j == 0 else 1.0
            lo, hi = halves(project(c0 + j * MXU_WIDTH))
            c_ref[rows, j * MXU_WIDTH:j * MXU_WIDTH + LANES] = rope(lo, scale)
            c_ref[rows, j * MXU_WIDTH + LANES:(j + 1) * MXU_WIDTH] = rope(hi, scale)
            yield
        c_ref[rows, 2 * MXU_WIDTH:] = project(c0 + 2 * MXU_WIDTH).astype(BF16)
        yield

        u = _gelu(project(A_COLS))
        yield
        v = _gelu(project(A_COLS + B_WIDTH))
        vn = _layer_norm(v, lng_ref[...], lnb_ref[...]).astype(BF16)
        yield
        for ci in range(INPROJ_SUB_TILE // B_CHUNK):
            rsl = slice(ci * B_CHUNK, (ci + 1) * B_CHUNK)
            out_rows = slice(rows.start + ci * B_CHUNK, rows.start + (ci + 1) * B_CHUNK)
            for hb in range(B_WIDTH // LANES):
                csl = slice(hb * LANES, (hb + 1) * LANES)
                vblk = vn[rsl, csl]
                mix = (_dot(w_tril[2 * hb], jnp.where(low_lanes, vblk, zero))
                       + _dot(w_tril[2 * hb + 1], jnp.where(low_lanes, zero, vblk))
                       + sb_ref[:, csl])
                ob_ref[out_rows, csl] = (u[rsl, csl] * mix).astype(BF16)
            yield

    _interleave_sub_tiles(sub_tile, INPROJ_SUB_TILE)


def _inproj(x, w_in, cos_t, sin_t, ln_g, ln_b, sgu_w, sgu_bias, layer):
    t = x.shape[0]
    rows = lambda n: pl.BlockSpec((ROW_TILE, n), lambda i: (i, 0))
    return pl.pallas_call(
        _inproj_kernel,
        grid=(t // ROW_TILE,),
        in_specs=[rows(D_MODEL), _resident((D_MODEL, IN_COLS), layer), rows(LANES), rows(LANES),
                  _resident((1, B_WIDTH), layer), _resident((1, B_WIDTH), layer),
                  _resident((B_GROUPS, B_CHUNK, B_CHUNK), layer),
                  _resident((B_CHUNK, B_WIDTH), layer)],
        out_specs=[rows(A_COLS), rows(B_WIDTH), rows(C_COLS)],
        out_shape=[jax.ShapeDtypeStruct((t, A_COLS), BF16),
                   jax.ShapeDtypeStruct((t, B_WIDTH), BF16),
                   jax.ShapeDtypeStruct((t, C_COLS), BF16)],
        compiler_params=_params("parallel"),
        name="mixer_inproj",
    )(x, w_in, cos_t, sin_t, ln_g, ln_b, sgu_w, sgu_bias)


def _tile_scores(q_m, k_ref, i, tile, causal):
    d0 = i * tile
    s = _dot_nt(q_m, k_ref[0:d0 + tile, :])
    s_d = jnp.where(causal, s[:, d0:], NEG_INF)
    m = jnp.max(s_d, axis=-1, keepdims=True)
    if i == 0:
        return [s_d], m
    s_p = s[:, :d0]
    return [s_p, s_d], jnp.maximum(m, jnp.max(s_p, axis=-1, keepdims=True))


def _tile_output(parts, m, v_ref, i, tile):
    p = [jnp.exp2(s - m) for s in parts]
    denom = jnp.sum(p[0], axis=-1, keepdims=True)
    for extra in p[1:]:
        denom = denom + jnp.sum(extra, axis=-1, keepdims=True)
    p = [x.astype(BF16) for x in p]
    p = p[0] if len(p) == 1 else jnp.concatenate(p, axis=1)
    return _dot(p, v_ref[0:(i + 1) * tile, :]) / denom


PIPELINE_AHEAD = 2


def _pipelined(items, score_stage, output_stage):
    outs = {}
    scored = {}
    for j in range(len(items) + PIPELINE_AHEAD):
        if j < len(items):
            scored[items[j]] = score_stage(items[j])
        if j >= PIPELINE_AHEAD:
            item = items[j - PIPELINE_AHEAD]
            outs[item] = output_stage(item, scored.pop(item))
    return outs


def _causal_mask(tile):
    row = lax.broadcasted_iota(jnp.int32, (tile, tile), 0)
    col = lax.broadcasted_iota(jnp.int32, (tile, tile), 1)
    return col <= row


def _low_lanes():
    return lax.broadcasted_iota(jnp.int32, (1, LANES), 1) < LANES // 2


def _map_query(q_ref, i, tile, mp):
    q = q_ref[i * tile:(i + 1) * tile, :]
    keep = _low_lanes() if mp == 0 else jnp.logical_not(_low_lanes())
    return jnp.where(keep, q, jnp.zeros((), q.dtype))


def _diff_attn_kernel(lam_ref, q_ref, k_ref, v_ref, g_ref, o_ref, *, lam_init):
    tile = ATTN_TILE
    n_tiles = q_ref.shape[0] // tile
    causal = _causal_mask(tile)
    lp = lam_ref[...]
    lam = (jnp.exp(jnp.sum(lp[0:1] * lp[1:2], axis=-1, keepdims=True))
           - jnp.exp(jnp.sum(lp[2:3] * lp[3:4], axis=-1, keepdims=True)) + lam_init)
    gain = g_ref[...] * (1.0 - lam_init)

    def scores(item):
        i, mp = item
        return _tile_scores(_map_query(q_ref, i, tile, mp), k_ref, i, tile, causal)

    def output(item, scored):
        return _tile_output(*scored, v_ref, item[0], tile)

    items = [(i, mp) for i in reversed(range(n_tiles)) for mp in range(2)]
    outs = _pipelined(items, scores, output)
    for i in range(n_tiles):
        o = outs[(i, 0)] - lam * outs[(i, 1)]
        ms = jnp.mean(o * o, axis=-1, keepdims=True)
        o_ref[i * tile:(i + 1) * tile, :] = (o * lax.rsqrt(ms + LN_EPS) * gain).astype(BF16)


def _diff_attn(qkv, lam_params, subln_g, lam_init, layer):
    bsz, seq, _ = qkv.shape
    head = lambda off: pl.BlockSpec((None, seq, LANES), lambda b, h: (b, 0, off + h))
    return pl.pallas_call(
        functools.partial(_diff_attn_kernel, lam_init=lam_init),
        grid=(bsz, A_HEADS),
        in_specs=[_resident((4, A_DH), layer), head(0), head(A_HEADS), head(2 * A_HEADS),
                  _resident((1, A_DV), layer)],
        out_specs=head(0),
        out_shape=jax.ShapeDtypeStruct((bsz, seq, A_HEADS * A_DV), BF16),
        compiler_params=_params("parallel", "parallel"),
        name="diff_attn",
    )(lam_params, qkv, qkv, qkv, subln_g)


MASKED_SCORE = -1e30


def _moba_penalties(gate, blk):
    n_blocks, seq = gate.shape
    block = lax.broadcasted_iota(jnp.int32, (n_blocks, seq), 0)
    is_past = (block + 1) * blk <= lax.broadcasted_iota(jnp.int32, (n_blocks, seq), 1)
    pen = jnp.zeros((n_blocks, seq), F32)
    for n in range(n_blocks - 1):
        g_n = gate[n:n + 1, :]
        ahead = jnp.where(block < n, jnp.where(gate >= g_n, 1.0, 0.0),
                          jnp.where(gate > g_n, 1.0, 0.0))
        rank = jnp.sum(jnp.where(is_past, ahead, 0.0), axis=0, keepdims=True)
        pen_n = jnp.where(rank < C_TOPK, 0.0, MASKED_SCORE)
        pen = jnp.where((block == n) & is_past, pen_n, pen)
    return pen


def _moba_kernel(q_ref, k_ref, v_ref, o_ref, kaug_ref, qaug_ref):
    blk = C_BLOCK
    seq = q_ref.shape[0]
    n_blocks = seq // blk
    causal = _causal_mask(blk)
    low = _low_lanes()

    q = q_ref[...]
    key = lax.broadcasted_iota(jnp.int32, (n_blocks, seq), 1)
    first_key = lax.broadcasted_iota(jnp.int32, (n_blocks, seq), 0) * blk
    in_block = (key >= first_key) & (key < first_key + blk)
    kbar = _dot(jnp.where(in_block, 1.0 / blk, 0.0).astype(BF16), k_ref[...]).astype(BF16)
    lane = lax.broadcasted_iota(jnp.int32, (1, LANES), 1)
    blk_row = lax.broadcasted_iota(jnp.int32, (n_blocks, LANES), 0)
    blk_lane = lax.broadcasted_iota(jnp.int32, (n_blocks, LANES), 1)
    zero = jnp.zeros((), BF16)
    for hd in range(2):
        own = low if hd == 0 else jnp.logical_not(low)
        lane0 = LANES // 2 if hd == 0 else 0
        for b in range(n_blocks):
            block_id = jnp.where(lane == lane0 + b, 1.0, 0.0).astype(BF16)
            rows = slice(b * blk, (b + 1) * blk)
            kaug_ref[hd, rows, :] = jnp.where(own, k_ref[rows, :], block_id)
        q_m = jnp.where(own, q, zero)
        pen = _moba_penalties(_dot_nt(kbar, q_m), blk).astype(BF16)
        place = jnp.where(blk_lane - lane0 == blk_row, 1.0, 0.0).astype(BF16)
        placed = lax.dot_general(pen, place, (((0,), (0,)), ((), ())),
                                 preferred_element_type=F32)
        qaug_ref[hd] = q_m + placed.astype(BF16)

    def scores(item):
        i, hd = item
        return _tile_scores(qaug_ref[hd, i * blk:(i + 1) * blk, :], kaug_ref.at[hd], i, blk, causal)

    def output(item, scored):
        return _tile_output(*scored, v_ref, item[0], blk)

    items = [(i, hd) for i in reversed(range(n_blocks)) for hd in range(2)]
    outs = _pipelined(items, scores, output)
    for i in range(n_blocks):
        o_ref[i * blk:(i + 1) * blk, :] = jnp.where(low, outs[(i, 0)], outs[(i, 1)]).astype(BF16)


def _moba(qkv):
    bsz, seq, _ = qkv.shape
    pairs = C_HEADS * C_DH // LANES
    head = lambda off: pl.BlockSpec((None, seq, LANES), lambda b, h: (b, 0, off + h))
    return pl.pallas_call(
        _moba_kernel,
        grid=(bsz, pairs),
        in_specs=[head(0), head(pairs), head(2 * pairs)],
        out_specs=head(0),
        out_shape=jax.ShapeDtypeStruct((bsz, seq, C_HEADS * C_DH), BF16),
        scratch_shapes=[pltpu.VMEM((2, seq, LANES), BF16), pltpu.VMEM((2, seq, LANES), BF16)],
        compiler_params=_params("parallel", "parallel"),
        name="moba_attn",
    )(qkv, qkv, qkv)


def _mem_kv_kernel(mem_ref, wk_ref, wv_ref, k_ref, v_ref):
    mb = mem_ref[...].astype(BF16)
    k_ref[...] = _dot(mb, wk_ref[...]).astype(BF16)
    v_ref[...] = _dot(mb, wv_ref[...]).astype(BF16)


def _mem_kv(mem2d, wk, wv):
    rows = mem2d.shape[0]
    w = pl.BlockSpec((None, D_MODEL, D_MODEL), lambda l, i: (l, 0, 0))
    out = pl.BlockSpec((None, ROW_TILE, D_MODEL), lambda l, i: (l, i, 0))
    shape = jax.ShapeDtypeStruct((DEPTH, rows, D_MODEL), BF16)
    return pl.pallas_call(
        _mem_kv_kernel,
        grid=(DEPTH, rows // ROW_TILE),
        in_specs=[pl.BlockSpec((ROW_TILE, D_MODEL), lambda l, i: (i, 0)), w, w],
        out_specs=[out, out],
        out_shape=[shape, shape],
        compiler_params=_params("parallel", "parallel"),
        name="mem_kv_proj",
    )(mem2d, wk, wv)


def _mixout_xattn_kernel(x_ref, oa_ref, ob_ref, oc_ref, wout_ref, g2_ref, b2_ref,
                         k_ref, v_ref, wq_ref, wo_ref, g_ref, b_ref, o_ref, cat_ref):
    na = A_HEADS * A_DV

    def sub_tile(rows):
        mix = (_dot(oa_ref[rows, :], wout_ref[:na, :])
               + _dot(ob_ref[rows, :], wout_ref[na:na + B_WIDTH, :])
               + _dot(oc_ref[rows, :], wout_ref[na + B_WIDTH:, :]))
        yield
        x = _layer_norm(DEEPNORM_ALPHA * x_ref[rows, :] + mix, g2_ref[...], b2_ref[...])
        q = (_dot(x.astype(BF16), wq_ref[...]) * (XA_DH ** -0.5 * LOG2_E)).astype(BF16)
        yield
        for h in range(XA_HEADS):
            sl = slice(h * XA_DH, (h + 1) * XA_DH)
            s = _dot_nt(q[:, sl], k_ref[:, sl])
            p = jnp.exp2(s - jnp.max(s, axis=-1, keepdims=True))
            denom = jnp.sum(p, axis=-1, keepdims=True)
            cat_ref[rows, sl] = (_dot(p.astype(BF16), v_ref[:, sl]) / denom).astype(BF16)
            yield
        y = _dot(cat_ref[rows, :], wo_ref[...])
        yield
        o_ref[rows, :] = _layer_norm(DEEPNORM_ALPHA * x + y, g_ref[...], b_ref[...])

    _interleave_sub_tiles(sub_tile)


def _mixout_xattn(x3, oa, ob, oc, w_out, g2, b2, k, v, wq, wo, g3, b3, layer):
    bsz, seq, _ = x3.shape
    rows = lambda n: pl.BlockSpec((None, ROW_TILE, n), lambda b_, i: (b_, i, 0))
    kv = pl.BlockSpec((None, None, MEM_LEN, D_MODEL), lambda b_, i: (layer, b_, 0, 0))
    weight = _resident((D_MODEL, D_MODEL), layer)
    vec = _resident((1, D_MODEL), layer)
    return pl.pallas_call(
        _mixout_xattn_kernel,
        grid=(bsz, seq // ROW_TILE),
        in_specs=[rows(D_MODEL), rows(A_HEADS * A_DV), rows(B_WIDTH), rows(C_HEADS * C_DH),
                  weight, vec, vec, kv, kv, weight, weight, vec, vec],
        out_specs=rows(D_MODEL),
        out_shape=jax.ShapeDtypeStruct((bsz, seq, D_MODEL), F32),
        scratch_shapes=[pltpu.VMEM((ROW_TILE, D_MODEL), BF16)],
        compiler_params=_params("parallel", "parallel"),
        name="mixout_xattn_ln",
    )(x3, oa, ob, oc, w_out, g2, b2, k, v, wq, wo, g3, b3)


def kernel(x, mem, positions, ffn1_w_gate, ffn1_w_up, ffn1_w_down, ln1_g, ln1_b, mix_w_in, diff_lq1, diff_lk1, diff_lq2, diff_lk2, diff_subln_g, sgu_ln_g, sgu_ln_b, sgu_w, sgu_b, mix_w_out, ln2_g, ln2_b, xa_wq, xa_wk, xa_wv, xa_wo, ln3_g, ln3_b, ffn2_w_gate, ffn2_w_up, ffn2_w_down, ln4_g, ln4_b):
    bsz, seq, d = x.shape
    t = bsz * seq
    bf = lambda w: w.astype(BF16)
    vec = lambda p: p.reshape(DEPTH, 1, -1)

    cos_t, sin_t = _rope_tables(positions)
    mem_k, mem_v = _mem_kv(mem.reshape(bsz * MEM_LEN, d), bf(xa_wk), bf(xa_wv))
    mem_k = mem_k.reshape(DEPTH, bsz, MEM_LEN, d)
    mem_v = mem_v.reshape(DEPTH, bsz, MEM_LEN, d)
    sgu_bias = jnp.repeat(jnp.swapaxes(sgu_b, 1, 2), B_DG, axis=2)
    lam_params = jnp.stack([diff_lq1, diff_lk1, diff_lq2, diff_lk2], axis=1)
    ffn1 = (bf(ffn1_w_gate), bf(ffn1_w_up), bf(ffn1_w_down), vec(ln1_g), vec(ln1_b))
    ffn2 = (bf(ffn2_w_gate), bf(ffn2_w_up), bf(ffn2_w_down), vec(ln4_g), vec(ln4_b))
    w_in, w_out, wq, wo = bf(mix_w_in), bf(mix_w_out), bf(xa_wq), bf(xa_wo)

    h = x.reshape(t, d)
    for l in range(DEPTH):
        lam_init = 0.8 - 0.6 * math.exp(-0.3 * l)
        h = _ffn_ln(h, *ffn1, l)
        qkv_a, ob, qkv_c = _inproj(h, w_in, cos_t, sin_t, vec(sgu_ln_g), vec(sgu_ln_b),
                                   sgu_w, sgu_bias, l)
        oa = _diff_attn(qkv_a.reshape(bsz, seq, A_COLS), lam_params, vec(diff_subln_g), lam_init, l)
        oc = _moba(qkv_c.reshape(bsz, seq, C_COLS))
        h = _mixout_xattn(h.reshape(bsz, seq, d), oa, ob.reshape(bsz, seq, B_WIDTH), oc,
                          w_out, vec(ln2_g), vec(ln2_b), mem_k, mem_v, wq, wo,
                          vec(ln3_g), vec(ln3_b), l).reshape(t, d)
        h = _ffn_ln(h, *ffn2, l)
    return h.reshape(bsz, seq, d)
```

```python
import functools
import math

import jax
import jax.numpy as jnp
from jax import lax
from jax.experimental import pallas as pl
from jax.experimental.pallas import tpu as pltpu

D_MODEL = 1024
DEPTH = 2
MEM_LEN = 256
A_HEADS = 4
A_DH = 64
A_DV = 2 * A_DH
B_GROUPS = 4
B_DG = 64
B_WIDTH = B_GROUPS * B_DG
B_CHUNK = 128
C_HEADS = 4
C_DH = 64
C_BLOCK = 256
C_TOPK = 3
A_COLS = 3 * A_HEADS * A_DV
C_COLS = 3 * C_HEADS * C_DH
IN_COLS = A_COLS + 2 * B_WIDTH + C_COLS
XA_HEADS = 4
XA_DH = D_MODEL // XA_HEADS
D_FF = 2816
ROPE_THETA = 10000.0
LN_EPS = 1e-5
DEEPNORM_ALPHA = (2.0 * DEPTH) ** 0.25

LANES = 128
MXU_WIDTH = 256
VMEM_LIMIT_BYTES = 56 * 1024 * 1024

ROW_TILE = 1024
SUB_TILE = 512
FF_CHUNK = MXU_WIDTH
ATTN_TILE = 256

F32 = jnp.float32
BF16 = jnp.bfloat16
NEG_INF = float("-inf")
LOG2_E = math.log2(math.e)


def _params(*sem):
    return pltpu.CompilerParams(dimension_semantics=sem, vmem_limit_bytes=VMEM_LIMIT_BYTES)


def _resident(shape, layer):
    index = (layer,) + (0,) * len(shape)
    return pl.BlockSpec((None,) + tuple(shape), lambda *_: index, pipeline_mode=pl.Buffered(1))


def _layer_norm(y, g, b, eps=LN_EPS):
    mu = jnp.mean(y, axis=-1, keepdims=True)
    d = y - mu
    var = jnp.mean(d * d, axis=-1, keepdims=True)
    return d * lax.rsqrt(var + eps) * g + b


def _interleave_sub_tiles(stages, sub_tile=None):
    sub_tile = sub_tile or SUB_TILE
    gens = [stages(slice(s * sub_tile, (s + 1) * sub_tile)) for s in range(ROW_TILE // sub_tile)]
    live = list(range(len(gens)))
    step = 0
    while live:
        for s in list(live):
            if step >= s:
                try:
                    next(gens[s])
                except StopIteration:
                    live.remove(s)
        step += 1


def _dot(a, b):
    return jnp.dot(a, b, preferred_element_type=F32)


def _dot_nt(a, b):
    return lax.dot_general(a, b, (((1,), (1,)), ((), ())), preferred_element_type=F32)


ROPE_PACK = LANES // (A_DH // 2)


def _rope_kernel(pos_ref, invf_ref, cos_ref, sin_ref):
    half = A_DH // 2
    ang = pos_ref[...] * invf_ref[...]
    lane = lax.broadcasted_iota(jnp.int32, (1, LANES), 1)
    group = lane // half
    sign = jnp.where(lane % A_DH < half, -1.0, 1.0).astype(F32)
    packed_rows = pos_ref.shape[0]
    for table, out_ref, scale in ((jnp.cos(ang), cos_ref, None), (jnp.sin(ang), sin_ref, sign)):
        rolled = [table] + [pltpu.roll(table, half * k, 1) for k in range(1, ROPE_PACK)]
        for j in range(ROPE_PACK):
            spread = rolled[(0 - j) % ROPE_PACK]
            for g in range(1, ROPE_PACK):
                spread = jnp.where(group == g, rolled[(g - j) % ROPE_PACK], spread)
            if scale is not None:
                spread = spread * scale
            out_ref[pl.ds(j, packed_rows, stride=ROPE_PACK), :] = spread


def _rope_tables(positions):
    t = positions.size
    half = A_DH // 2
    inv_freq = 1.0 / (ROPE_THETA ** (jnp.arange(0, A_DH, 2, dtype=F32) / A_DH))
    invf = jnp.tile(inv_freq, ROPE_PACK).reshape(1, LANES)
    pos = jnp.repeat(positions.astype(F32).reshape(t // ROPE_PACK, ROPE_PACK), half, axis=1)
    tm = 2048
    out = jax.ShapeDtypeStruct((t, LANES), F32)
    return pl.pallas_call(
        _rope_kernel,
        grid=(t // tm,),
        in_specs=[pl.BlockSpec((tm // ROPE_PACK, LANES), lambda i: (i, 0)),
                  pl.BlockSpec((1, LANES), lambda i: (0, 0))],
        out_specs=[pl.BlockSpec((tm, LANES), lambda i: (i, 0))] * 2,
        out_shape=[out, out],
        compiler_params=_params("parallel"),
        name="rope_tables",
    )(pos, invf)


def _ffn_ln_kernel(x_ref, wg_ref, wu_ref, wd_ref, g_ref, b_ref, o_ref, acc_ref):
    def sub_tile(rows):
        xb = x_ref[rows, :].astype(BF16)
        for c in range(D_FF // FF_CHUNK):
            cols = slice(c * FF_CHUNK, (c + 1) * FF_CHUNK)
            gate = _dot(xb, wg_ref[:, cols])
            up = _dot(xb, wu_ref[:, cols])
            act = (gate * jax.nn.sigmoid(gate) * up).astype(BF16)
            part = _dot(act, wd_ref[cols, :])
            if c == 0:
                acc_ref[rows, :] = (2.0 * DEEPNORM_ALPHA) * x_ref[rows, :] + part
            else:
                acc_ref[rows, :] += part
            yield
        o_ref[rows, :] = _layer_norm(acc_ref[rows, :], g_ref[...], b_ref[...], eps=4.0 * LN_EPS)

    _interleave_sub_tiles(sub_tile)


def _ffn_ln(x, wg, wu, wd, g, b, layer):
    t = x.shape[0]
    row = pl.BlockSpec((ROW_TILE, D_MODEL), lambda i: (i, 0))
    return pl.pallas_call(
        _ffn_ln_kernel,
        grid=(t // ROW_TILE,),
        in_specs=[row, _resident((D_MODEL, D_FF), layer), _resident((D_MODEL, D_FF), layer),
                  _resident((D_FF, D_MODEL), layer), _resident((1, D_MODEL), layer),
                  _resident((1, D_MODEL), layer)],
        out_specs=row,
        out_shape=jax.ShapeDtypeStruct((t, D_MODEL), F32),
        scratch_shapes=[pltpu.VMEM((ROW_TILE, D_MODEL), F32)],
        compiler_params=_params("parallel"),
        name="ffn_ln",
    )(x, wg, wu, wd, g, b)


def _gelu(y):
    return 0.5 * y * (1.0 + lax.erf(y * (2.0 ** -0.5)))


def _inproj_kernel(x_ref, w_ref, cos_ref, sin_ref, lng_ref, lnb_ref, sw_ref, sb_ref,
                   a_ref, ob_ref, c_ref, u_ref):
    lane = lax.broadcasted_iota(jnp.int32, (1, LANES), 1)
    first_half = lane % A_DH < A_DH // 2
    low_lanes = lane < B_DG
    tri_r = lax.broadcasted_iota(jnp.int32, (B_CHUNK, B_CHUNK), 0)
    tri_c = lax.broadcasted_iota(jnp.int32, (B_CHUNK, B_CHUNK), 1)
    w_tril = [jnp.where(tri_r >= tri_c, sw_ref[g], 0.0).astype(BF16) for g in range(B_GROUPS)]
    zero = jnp.zeros((), BF16)
    q_scale = A_DH ** -0.5 * LOG2_E

    xb = x_ref[...].astype(BF16)

    def project(col):
        return _dot(xb, w_ref[:, col:col + MXU_WIDTH])

    def rope_to(out_ref, out_col, scale):
        def epilogue(_, y):
            for hb in range(MXU_WIDTH // LANES):
                yh = y[:, hb * LANES:(hb + 1) * LANES]
                swapped = jnp.where(first_half, pltpu.roll(yh, LANES - A_DH // 2, 1),
                                    pltpu.roll(yh, A_DH // 2, 1))
                roped = (yh * cos_ref[...] + swapped * sin_ref[...]) * scale
                out_ref[:, out_col + hb * LANES:out_col + (hb + 1) * LANES] = roped.astype(BF16)
        return epilogue

    def plain_to(out_ref, out_col):
        def epilogue(_, y):
            out_ref[:, out_col:out_col + MXU_WIDTH] = y.astype(BF16)
        return epilogue

    def gate_input(_, y):
        u_ref[...] = _gelu(y)

    def spatial_gating(_, y):
        vn = _layer_norm(_gelu(y), lng_ref[...], lnb_ref[...]).astype(BF16)
        for ci in range(ROW_TILE // B_CHUNK):
            rsl = slice(ci * B_CHUNK, (ci + 1) * B_CHUNK)
            for hb in range(B_WIDTH // LANES):
                csl = slice(hb * LANES, (hb + 1) * LANES)
                vblk = vn[rsl, csl]
                mix = (_dot(w_tril[2 * hb], jnp.where(low_lanes, vblk, zero))
                       + _dot(w_tril[2 * hb + 1], jnp.where(low_lanes, zero, vblk))
                       + sb_ref[:, csl])
                ob_ref[rsl, csl] = (u_ref[rsl, csl] * mix).astype(BF16)

    c0 = A_COLS + 2 * B_WIDTH
    blocks = [(A_COLS, gate_input), (A_COLS + B_WIDTH, spatial_gating)]
    for j in range(2 * A_HEADS * A_DV // MXU_WIDTH):
        col = j * MXU_WIDTH
        blocks.append((col, rope_to(a_ref, col, q_scale if col < A_HEADS * A_DV else 1.0)))
    for j in range(A_HEADS * A_DV // MXU_WIDTH):
        col = 2 * A_HEADS * A_DV + j * MXU_WIDTH
        blocks.append((col, plain_to(a_ref, col)))
    blocks.append((c0, rope_to(c_ref, 0, C_DH ** -0.5 * LOG2_E)))
    blocks.append((c0 + MXU_WIDTH, rope_to(c_ref, MXU_WIDTH, 1.0)))
    blocks.append((c0 + 2 * MXU_WIDTH, plain_to(c_ref, 2 * MXU_WIDTH)))
    epilogues = dict(blocks)
    _pipelined([col for col, _ in blocks], project, lambda col, y: epilogues[col](col, y), ahead=1)


def _inproj(x, w_in, cos_t, sin_t, ln_g, ln_b, sgu_w, sgu_bias, layer):
    t = x.shape[0]
    rows = lambda n: pl.BlockSpec((ROW_TILE, n), lambda i: (i, 0))
    return pl.pallas_call(
        _inproj_kernel,
        grid=(t // ROW_TILE,),
        in_specs=[rows(D_MODEL), _resident((D_MODEL, IN_COLS), layer), rows(LANES), rows(LANES),
                  _resident((1, B_WIDTH), layer), _resident((1, B_WIDTH), layer),
                  _resident((B_GROUPS, B_CHUNK, B_CHUNK), layer),
                  _resident((B_CHUNK, B_WIDTH), layer)],
        out_specs=[rows(A_COLS), rows(B_WIDTH), rows(C_COLS)],
        out_shape=[jax.ShapeDtypeStruct((t, A_COLS), BF16),
                   jax.ShapeDtypeStruct((t, B_WIDTH), BF16),
                   jax.ShapeDtypeStruct((t, C_COLS), BF16)],
        scratch_shapes=[pltpu.VMEM((ROW_TILE, B_WIDTH), F32)],
        compiler_params=_params("parallel"),
        name="mixer_inproj",
    )(x, w_in, cos_t, sin_t, ln_g, ln_b, sgu_w, sgu_bias)


def _tile_scores(q_m, k_ref, i, tile, causal):
    d0 = i * tile
    s = _dot_nt(q_m, k_ref[0:d0 + tile, :])
    s_d = jnp.where(causal, s[:, d0:], NEG_INF)
    m = jnp.max(s_d, axis=-1, keepdims=True)
    if i == 0:
        return [s_d], m
    s_p = s[:, :d0]
    return [s_p, s_d], jnp.maximum(m, jnp.max(s_p, axis=-1, keepdims=True))


def _tile_output(parts, m, v_ref, i, tile):
    p = [jnp.exp2(s - m) for s in parts]
    denom = jnp.sum(p[0], axis=-1, keepdims=True)
    for extra in p[1:]:
        denom = denom + jnp.sum(extra, axis=-1, keepdims=True)
    p = [x.astype(BF16) for x in p]
    p = p[0] if len(p) == 1 else jnp.concatenate(p, axis=1)
    return _dot(p, v_ref[0:(i + 1) * tile, :]) / denom


PIPELINE_AHEAD = 2


def _pipelined(items, matmul_stage, vector_stage, ahead=PIPELINE_AHEAD):
    outs = {}
    pending = {}
    for j in range(len(items) + ahead):
        if j < len(items):
            pending[items[j]] = matmul_stage(items[j])
        if j >= ahead:
            item = items[j - ahead]
            outs[item] = vector_stage(item, pending.pop(item))
    return outs


def _causal_mask(tile):
    row = lax.broadcasted_iota(jnp.int32, (tile, tile), 0)
    col = lax.broadcasted_iota(jnp.int32, (tile, tile), 1)
    return col <= row


def _low_lanes():
    return lax.broadcasted_iota(jnp.int32, (1, LANES), 1) < LANES // 2


def _map_query(q_ref, i, tile, mp):
    q = q_ref[i * tile:(i + 1) * tile, :]
    keep = _low_lanes() if mp == 0 else jnp.logical_not(_low_lanes())
    return jnp.where(keep, q, jnp.zeros((), q.dtype))


def _diff_attn_kernel(lam_ref, q_ref, k_ref, v_ref, g_ref, o_ref, *, lam_init):
    tile = ATTN_TILE
    n_tiles = q_ref.shape[0] // tile
    causal = _causal_mask(tile)
    lp = lam_ref[...]
    lam = (jnp.exp(jnp.sum(lp[0:1] * lp[1:2], axis=-1, keepdims=True))
           - jnp.exp(jnp.sum(lp[2:3] * lp[3:4], axis=-1, keepdims=True)) + lam_init)
    gain = g_ref[...] * (1.0 - lam_init)

    def scores(item):
        i, mp = item
        return _tile_scores(_map_query(q_ref, i, tile, mp), k_ref, i, tile, causal)

    def output(item, scored):
        return _tile_output(*scored, v_ref, item[0], tile)

    items = [(i, mp) for i in reversed(range(n_tiles)) for mp in range(2)]
    outs = _pipelined(items, scores, output)
    for i in range(n_tiles):
        o = outs[(i, 0)] - lam * outs[(i, 1)]
        ms = jnp.mean(o * o, axis=-1, keepdims=True)
        o_ref[i * tile:(i + 1) * tile, :] = (o * lax.rsqrt(ms + LN_EPS) * gain).astype(BF16)


def _diff_attn(qkv, lam_params, subln_g, lam_init, layer):
    bsz, seq, _ = qkv.shape
    head = lambda off: pl.BlockSpec((None, seq, LANES), lambda b, h: (b, 0, off + h))
    return pl.pallas_call(
        functools.partial(_diff_attn_kernel, lam_init=lam_init),
        grid=(bsz, A_HEADS),
        in_specs=[_resident((4, A_DH), layer), head(0), head(A_HEADS), head(2 * A_HEADS),
                  _resident((1, A_DV), layer)],
        out_specs=head(0),
        out_shape=jax.ShapeDtypeStruct((bsz, seq, A_HEADS * A_DV), BF16),
        compiler_params=_params("parallel", "parallel"),
        name="diff_attn",
    )(lam_params, qkv, qkv, qkv, subln_g)


MASKED_SCORE = -1e30


def _moba_penalties(gate, blk):
    n_blocks, seq = gate.shape
    block = lax.broadcasted_iota(jnp.int32, (n_blocks, seq), 0)
    is_past = (block + 1) * blk <= lax.broadcasted_iota(jnp.int32, (n_blocks, seq), 1)
    pen = jnp.zeros((n_blocks, seq), F32)
    for n in range(n_blocks - 1):
        g_n = gate[n:n + 1, :]
        ahead = jnp.where(block < n, jnp.where(gate >= g_n, 1.0, 0.0),
                          jnp.where(gate > g_n, 1.0, 0.0))
        rank = jnp.sum(jnp.where(is_past, ahead, 0.0), axis=0, keepdims=True)
        pen_n = jnp.where(rank < C_TOPK, 0.0, MASKED_SCORE)
        pen = jnp.where((block == n) & is_past, pen_n, pen)
    return pen


def _moba_kernel(q_ref, k_ref, v_ref, o_ref, kaug_ref, qaug_ref):
    blk = C_BLOCK
    seq = q_ref.shape[0]
    n_blocks = seq // blk
    causal = _causal_mask(blk)
    low = _low_lanes()

    q = q_ref[...]
    key = lax.broadcasted_iota(jnp.int32, (n_blocks, seq), 1)
    first_key = lax.broadcasted_iota(jnp.int32, (n_blocks, seq), 0) * blk
    in_block = (key >= first_key) & (key < first_key + blk)
    kbar = _dot(jnp.where(in_block, 1.0 / blk, 0.0).astype(BF16), k_ref[...]).astype(BF16)
    lane = lax.broadcasted_iota(jnp.int32, (1, LANES), 1)
    blk_row = lax.broadcasted_iota(jnp.int32, (n_blocks, LANES), 0)
    blk_lane = lax.broadcasted_iota(jnp.int32, (n_blocks, LANES), 1)
    zero = jnp.zeros((), BF16)
    for hd in range(2):
        own = low if hd == 0 else jnp.logical_not(low)
        lane0 = LANES // 2 if hd == 0 else 0
        for b in range(n_blocks):
            block_id = jnp.where(lane == lane0 + b, 1.0, 0.0).astype(BF16)
            rows = slice(b * blk, (b + 1) * blk)
            kaug_ref[hd, rows, :] = jnp.where(own, k_ref[rows, :], block_id)
        q_m = jnp.where(own, q, zero)
        pen = _moba_penalties(_dot_nt(kbar, q_m), blk).astype(BF16)
        place = jnp.where(blk_lane - lane0 == blk_row, 1.0, 0.0).astype(BF16)
        placed = lax.dot_general(pen, place, (((0,), (0,)), ((), ())),
                                 preferred_element_type=F32)
        qaug_ref[hd] = q_m + placed.astype(BF16)

    def scores(item):
        i, hd = item
        return _tile_scores(qaug_ref[hd, i * blk:(i + 1) * blk, :], kaug_ref.at[hd], i, blk, causal)

    def output(item, scored):
        return _tile_output(*scored, v_ref, item[0], blk)

    items = [(i, hd) for i in reversed(range(n_blocks)) for hd in range(2)]
    outs = _pipelined(items, scores, output)
    for i in range(n_blocks):
        o_ref[i * blk:(i + 1) * blk, :] = jnp.where(low, outs[(i, 0)], outs[(i, 1)]).astype(BF16)


def _moba(qkv):
    bsz, seq, _ = qkv.shape
    pairs = C_HEADS * C_DH // LANES
    head = lambda off: pl.BlockSpec((None, seq, LANES), lambda b, h: (b, 0, off + h))
    return pl.pallas_call(
        _moba_kernel,
        grid=(bsz, pairs),
        in_specs=[head(0), head(pairs), head(2 * pairs)],
        out_specs=head(0),
        out_shape=jax.ShapeDtypeStruct((bsz, seq, C_HEADS * C_DH), BF16),
        scratch_shapes=[pltpu.VMEM((2, seq, LANES), BF16), pltpu.VMEM((2, seq, LANES), BF16)],
        compiler_params=_params("parallel", "parallel"),
        name="moba_attn",
    )(qkv, qkv, qkv)


def _mem_kv_kernel(mem_ref, wk_ref, wv_ref, k_ref, v_ref):
    mb = mem_ref[...].astype(BF16)
    k_ref[...] = _dot(mb, wk_ref[...]).astype(BF16)
    v_ref[...] = _dot(mb, wv_ref[...]).astype(BF16)


def _mem_kv(mem2d, wk, wv):
    rows = mem2d.shape[0]
    w = pl.BlockSpec((None, D_MODEL, D_MODEL), lambda l, i: (l, 0, 0))
    out = pl.BlockSpec((None, ROW_TILE, D_MODEL), lambda l, i: (l, i, 0))
    shape = jax.ShapeDtypeStruct((DEPTH, rows, D_MODEL), BF16)
    return pl.pallas_call(
        _mem_kv_kernel,
        grid=(DEPTH, rows // ROW_TILE),
        in_specs=[pl.BlockSpec((ROW_TILE, D_MODEL), lambda l, i: (i, 0)), w, w],
        out_specs=[out, out],
        out_shape=[shape, shape],
        compiler_params=_params("parallel", "parallel"),
        name="mem_kv_proj",
    )(mem2d, wk, wv)


def _mixout_xattn_kernel(x_ref, oa_ref, ob_ref, oc_ref, wout_ref, g2_ref, b2_ref,
                         k_ref, v_ref, wq_ref, wo_ref, g_ref, b_ref, o_ref, cat_ref):
    na = A_HEADS * A_DV

    def sub_tile(rows):
        mix = (_dot(oa_ref[rows, :], wout_ref[:na, :])
               + _dot(ob_ref[rows, :], wout_ref[na:na + B_WIDTH, :])
               + _dot(oc_ref[rows, :], wout_ref[na + B_WIDTH:, :]))
        yield
        x = _layer_norm(DEEPNORM_ALPHA * x_ref[rows, :] + mix, g2_ref[...], b2_ref[...])
        q = (_dot(x.astype(BF16), wq_ref[...]) * (XA_DH ** -0.5 * LOG2_E)).astype(BF16)
        yield
        for h in range(XA_HEADS):
            sl = slice(h * XA_DH, (h + 1) * XA_DH)
            s = _dot_nt(q[:, sl], k_ref[:, sl])
            p = jnp.exp2(s - jnp.max(s, axis=-1, keepdims=True))
            denom = jnp.sum(p, axis=-1, keepdims=True)
            cat_ref[rows, sl] = (_dot(p.astype(BF16), v_ref[:, sl]) / denom).astype(BF16)
            yield
        y = _dot(cat_ref[rows, :], wo_ref[...])
        yield
        o_ref[rows, :] = _layer_norm(DEEPNORM_ALPHA * x + y, g_ref[...], b_ref[...])

    _interleave_sub_tiles(sub_tile)


def _mixout_xattn(x3, oa, ob, oc, w_out, g2, b2, k, v, wq, wo, g3, b3, layer):
    bsz, seq, _ = x3.shape
    rows = lambda n: pl.BlockSpec((None, ROW_TILE, n), lambda b_, i: (b_, i, 0))
    kv = pl.BlockSpec((None, None, MEM_LEN, D_MODEL), lambda b_, i: (layer, b_, 0, 0))
    weight = _resident((D_MODEL, D_MODEL), layer)
    vec = _resident((1, D_MODEL), layer)
    return pl.pallas_call(
        _mixout_xattn_kernel,
        grid=(bsz, seq // ROW_TILE),
        in_specs=[rows(D_MODEL), rows(A_HEADS * A_DV), rows(B_WIDTH), rows(C_HEADS * C_DH),
                  weight, vec, vec, kv, kv, weight, weight, vec, vec],
        out_specs=rows(D_MODEL),
        out_shape=jax.ShapeDtypeStruct((bsz, seq, D_MODEL), F32),
        scratch_shapes=[pltpu.VMEM((ROW_TILE, D_MODEL), BF16)],
        compiler_params=_params("parallel", "parallel"),
        name="mixout_xattn_ln",
    )(x3, oa, ob, oc, w_out, g2, b2, k, v, wq, wo, g3, b3)


def kernel(x, mem, positions, ffn1_w_gate, ffn1_w_up, ffn1_w_down, ln1_g, ln1_b, mix_w_in, diff_lq1, diff_lk1, diff_lq2, diff_lk2, diff_subln_g, sgu_ln_g, sgu_ln_b, sgu_w, sgu_b, mix_w_out, ln2_g, ln2_b, xa_wq, xa_wk, xa_wv, xa_wo, ln3_g, ln3_b, ffn2_w_gate, ffn2_w_up, ffn2_w_down, ln4_g, ln4_b):
    bsz, seq, d = x.shape
    t = bsz * seq
    bf = lambda w: w.astype(BF16)
    vec = lambda p: p.reshape(DEPTH, 1, -1)

    cos_t, sin_t = _rope_tables(positions)
    mem_k, mem_v = _mem_kv(mem.reshape(bsz * MEM_LEN, d), bf(xa_wk), bf(xa_wv))
    mem_k = mem_k.reshape(DEPTH, bsz, MEM_LEN, d)
    mem_v = mem_v.reshape(DEPTH, bsz, MEM_LEN, d)
    sgu_bias = jnp.repeat(jnp.swapaxes(sgu_b, 1, 2), B_DG, axis=2)
    lam_params = jnp.stack([diff_lq1, diff_lk1, diff_lq2, diff_lk2], axis=1)
    ffn1 = (bf(ffn1_w_gate), bf(ffn1_w_up), bf(ffn1_w_down), vec(ln1_g), vec(ln1_b))
    ffn2 = (bf(ffn2_w_gate), bf(ffn2_w_up), bf(ffn2_w_down), vec(ln4_g), vec(ln4_b))
    w_in, w_out, wq, wo = bf(mix_w_in), bf(mix_w_out), bf(xa_wq), bf(xa_wo)

    h = x.reshape(t, d)
    for l in range(DEPTH):
        lam_init = 0.8 - 0.6 * math.exp(-0.3 * l)
        h = _ffn_ln(h, *ffn1, l)
        qkv_a, ob, qkv_c = _inproj(h, w_in, cos_t, sin_t, vec(sgu_ln_g), vec(sgu_ln_b),
                                   sgu_w, sgu_bias, l)
        oa = _diff_attn(qkv_a.reshape(bsz, seq, A_COLS), lam_params, vec(diff_subln_g), lam_init, l)
        oc = _moba(qkv_c.reshape(bsz, seq, C_COLS))
        h = _mixout_xattn(h.reshape(bsz, seq, d), oa, ob.reshape(bsz, seq, B_WIDTH), oc,
                          w_out, vec(ln2_g), vec(ln2_b), mem_k, mem_v, wq, wo,
                          vec(ln3_g), vec(ln3_b), l).reshape(t, d)
        h = _ffn_ln(h, *ffn2, l)
    return h.reshape(bsz, seq, d)
```

```python
import functools
import math

import jax
import jax.numpy as jnp
from jax import lax
from jax.experimental import pallas as pl
from jax.experimental.pallas import tpu as pltpu

D_MODEL = 1024
DEPTH = 2
MEM_LEN = 256
A_HEADS = 4
A_DH = 64
A_DV = 2 * A_DH
B_GROUPS = 4
B_DG = 64
B_WIDTH = B_GROUPS * B_DG
B_CHUNK = 128
C_HEADS = 4
C_DH = 64
C_BLOCK = 256
C_TOPK = 3
A_COLS = 3 * A_HEADS * A_DV
C_COLS = 3 * C_HEADS * C_DH
IN_COLS = A_COLS + 2 * B_WIDTH + C_COLS
XA_HEADS = 4
XA_DH = D_MODEL // XA_HEADS
D_FF = 2816
ROPE_THETA = 10000.0
LN_EPS = 1e-5
DEEPNORM_ALPHA = (2.0 * DEPTH) ** 0.25

LANES = 128
MXU_WIDTH = 256
VMEM_LIMIT_BYTES = 56 * 1024 * 1024

ROW_TILE = 1024
SUB_TILE = 512
FF_CHUNK = MXU_WIDTH
ATTN_TILE = 256

F32 = jnp.float32
BF16 = jnp.bfloat16
NEG_INF = float("-inf")
LOG2_E = math.log2(math.e)


def _params(*sem):
    return pltpu.CompilerParams(dimension_semantics=sem, vmem_limit_bytes=VMEM_LIMIT_BYTES)


def _resident(shape, layer):
    index = (layer,) + (0,) * len(shape)
    return pl.BlockSpec((None,) + tuple(shape), lambda *_: index, pipeline_mode=pl.Buffered(1))


def _layer_norm(y, g, b, eps=LN_EPS):
    mu = jnp.mean(y, axis=-1, keepdims=True)
    d = y - mu
    var = jnp.mean(d * d, axis=-1, keepdims=True)
    return d * lax.rsqrt(var + eps) * g + b


def _interleave_sub_tiles(stages, sub_tile=None, skew=1):
    sub_tile = sub_tile or SUB_TILE
    gens = [stages(slice(s * sub_tile, (s + 1) * sub_tile)) for s in range(ROW_TILE // sub_tile)]
    live = list(range(len(gens)))
    step = 0
    while live:
        for s in list(live):
            if step >= s * skew:
                try:
                    next(gens[s])
                except StopIteration:
                    live.remove(s)
        step += 1


def _dot(a, b):
    return jnp.dot(a, b, preferred_element_type=F32)


def _dot_nt(a, b):
    return lax.dot_general(a, b, (((1,), (1,)), ((), ())), preferred_element_type=F32)


ROPE_PACK = LANES // (A_DH // 2)


def _rope_kernel(pos_ref, invf_ref, cos_ref, sin_ref):
    half = A_DH // 2
    ang = pos_ref[...] * invf_ref[...]
    lane = lax.broadcasted_iota(jnp.int32, (1, LANES), 1)
    group = lane // half
    sign = jnp.where(lane % A_DH < half, -1.0, 1.0).astype(F32)
    packed_rows = pos_ref.shape[0]
    for table, out_ref, scale in ((jnp.cos(ang), cos_ref, None), (jnp.sin(ang), sin_ref, sign)):
        rolled = [table] + [pltpu.roll(table, half * k, 1) for k in range(1, ROPE_PACK)]
        for j in range(ROPE_PACK):
            spread = rolled[(0 - j) % ROPE_PACK]
            for g in range(1, ROPE_PACK):
                spread = jnp.where(group == g, rolled[(g - j) % ROPE_PACK], spread)
            if scale is not None:
                spread = spread * scale
            out_ref[pl.ds(j, packed_rows, stride=ROPE_PACK), :] = spread


def _rope_tables(positions):
    t = positions.size
    half = A_DH // 2
    inv_freq = 1.0 / (ROPE_THETA ** (jnp.arange(0, A_DH, 2, dtype=F32) / A_DH))
    invf = jnp.tile(inv_freq, ROPE_PACK).reshape(1, LANES)
    pos = jnp.repeat(positions.astype(F32).reshape(t // ROPE_PACK, ROPE_PACK), half, axis=1)
    tm = 2048
    out = jax.ShapeDtypeStruct((t, LANES), F32)
    return pl.pallas_call(
        _rope_kernel,
        grid=(t // tm,),
        in_specs=[pl.BlockSpec((tm // ROPE_PACK, LANES), lambda i: (i, 0)),
                  pl.BlockSpec((1, LANES), lambda i: (0, 0))],
        out_specs=[pl.BlockSpec((tm, LANES), lambda i: (i, 0))] * 2,
        out_shape=[out, out],
        compiler_params=_params("parallel"),
        name="rope_tables",
    )(pos, invf)


def _ffn_ln_kernel(x_ref, wg_ref, wu_ref, wd_ref, g_ref, b_ref, o_ref, acc_ref):
    def sub_tile(rows):
        xb = x_ref[rows, :].astype(BF16)
        for c in range(D_FF // FF_CHUNK):
            cols = slice(c * FF_CHUNK, (c + 1) * FF_CHUNK)
            gate = _dot(xb, wg_ref[:, cols])
            up = _dot(xb, wu_ref[:, cols])
            act = (gate * jax.nn.sigmoid(gate) * up).astype(BF16)
            part = _dot(act, wd_ref[cols, :])
            if c == 0:
                acc_ref[rows, :] = (2.0 * DEEPNORM_ALPHA) * x_ref[rows, :] + part
            else:
                acc_ref[rows, :] += part
            yield
        o_ref[rows, :] = _layer_norm(acc_ref[rows, :], g_ref[...], b_ref[...], eps=4.0 * LN_EPS)

    _interleave_sub_tiles(sub_tile)


def _ffn_ln(x, wg, wu, wd, g, b, layer):
    t = x.shape[0]
    row = pl.BlockSpec((ROW_TILE, D_MODEL), lambda i: (i, 0))
    return pl.pallas_call(
        _ffn_ln_kernel,
        grid=(t // ROW_TILE,),
        in_specs=[row, _resident((D_MODEL, D_FF), layer), _resident((D_MODEL, D_FF), layer),
                  _resident((D_FF, D_MODEL), layer), _resident((1, D_MODEL), layer),
                  _resident((1, D_MODEL), layer)],
        out_specs=row,
        out_shape=jax.ShapeDtypeStruct((t, D_MODEL), F32),
        scratch_shapes=[pltpu.VMEM((ROW_TILE, D_MODEL), F32)],
        compiler_params=_params("parallel"),
        name="ffn_ln",
    )(x, wg, wu, wd, g, b)


def _gelu(y):
    return 0.5 * y * (1.0 + lax.erf(y * (2.0 ** -0.5)))


def _inproj_kernel(x_ref, w_ref, cos_ref, sin_ref, lng_ref, lnb_ref, sw_ref, sb_ref,
                   a_ref, ob_ref, c_ref, u_ref):
    lane = lax.broadcasted_iota(jnp.int32, (1, LANES), 1)
    first_half = lane % A_DH < A_DH // 2
    low_lanes = lane < B_DG
    tri_r = lax.broadcasted_iota(jnp.int32, (B_CHUNK, B_CHUNK), 0)
    tri_c = lax.broadcasted_iota(jnp.int32, (B_CHUNK, B_CHUNK), 1)
    w_tril = [jnp.where(tri_r >= tri_c, sw_ref[g], 0.0).astype(BF16) for g in range(B_GROUPS)]
    zero = jnp.zeros((), BF16)
    q_scale = A_DH ** -0.5 * LOG2_E

    xb = x_ref[...].astype(BF16)

    def project(col):
        return _dot(xb, w_ref[:, col:col + MXU_WIDTH])

    def rope_to(out_ref, out_col, scale):
        def epilogue(_, y):
            for hb in range(MXU_WIDTH // LANES):
                yh = y[:, hb * LANES:(hb + 1) * LANES]
                swapped = jnp.where(first_half, pltpu.roll(yh, LANES - A_DH // 2, 1),
                                    pltpu.roll(yh, A_DH // 2, 1))
                roped = (yh * cos_ref[...] + swapped * sin_ref[...]) * scale
                out_ref[:, out_col + hb * LANES:out_col + (hb + 1) * LANES] = roped.astype(BF16)
        return epilogue

    def plain_to(out_ref, out_col):
        def epilogue(_, y):
            out_ref[:, out_col:out_col + MXU_WIDTH] = y.astype(BF16)
        return epilogue

    def gate_input(_, y):
        u_ref[...] = _gelu(y)

    def spatial_gating(_, y):
        vn = _layer_norm(_gelu(y), lng_ref[...], lnb_ref[...]).astype(BF16)
        for ci in range(ROW_TILE // B_CHUNK):
            rsl = slice(ci * B_CHUNK, (ci + 1) * B_CHUNK)
            for hb in range(B_WIDTH // LANES):
                csl = slice(hb * LANES, (hb + 1) * LANES)
                vblk = vn[rsl, csl]
                mix = (_dot(w_tril[2 * hb], jnp.where(low_lanes, vblk, zero))
                       + _dot(w_tril[2 * hb + 1], jnp.where(low_lanes, zero, vblk))
                       + sb_ref[:, csl])
                ob_ref[rsl, csl] = (u_ref[rsl, csl] * mix).astype(BF16)

    c0 = A_COLS + 2 * B_WIDTH
    blocks = [(A_COLS, gate_input), (A_COLS + B_WIDTH, spatial_gating)]
    for j in range(2 * A_HEADS * A_DV // MXU_WIDTH):
        col = j * MXU_WIDTH
        blocks.append((col, rope_to(a_ref, col, q_scale if col < A_HEADS * A_DV else 1.0)))
    for j in range(A_HEADS * A_DV // MXU_WIDTH):
        col = 2 * A_HEADS * A_DV + j * MXU_WIDTH
        blocks.append((col, plain_to(a_ref, col)))
    blocks.append((c0, rope_to(c_ref, 0, C_DH ** -0.5 * LOG2_E)))
    blocks.append((c0 + MXU_WIDTH, rope_to(c_ref, MXU_WIDTH, 1.0)))
    blocks.append((c0 + 2 * MXU_WIDTH, plain_to(c_ref, 2 * MXU_WIDTH)))
    epilogues = dict(blocks)
    _pipelined([col for col, _ in blocks], project, lambda col, y: epilogues[col](col, y), ahead=1)


def _inproj(x, w_in, cos_t, sin_t, ln_g, ln_b, sgu_w, sgu_bias, layer):
    t = x.shape[0]
    rows = lambda n: pl.BlockSpec((ROW_TILE, n), lambda i: (i, 0))
    return pl.pallas_call(
        _inproj_kernel,
        grid=(t // ROW_TILE,),
        in_specs=[rows(D_MODEL), _resident((D_MODEL, IN_COLS), layer), rows(LANES), rows(LANES),
                  _resident((1, B_WIDTH), layer), _resident((1, B_WIDTH), layer),
                  _resident((B_GROUPS, B_CHUNK, B_CHUNK), layer),
                  _resident((B_CHUNK, B_WIDTH), layer)],
        out_specs=[rows(A_COLS), rows(B_WIDTH), rows(C_COLS)],
        out_shape=[jax.ShapeDtypeStruct((t, A_COLS), BF16),
                   jax.ShapeDtypeStruct((t, B_WIDTH), BF16),
                   jax.ShapeDtypeStruct((t, C_COLS), BF16)],
        scratch_shapes=[pltpu.VMEM((ROW_TILE, B_WIDTH), F32)],
        compiler_params=_params("parallel"),
        name="mixer_inproj",
    )(x, w_in, cos_t, sin_t, ln_g, ln_b, sgu_w, sgu_bias)


def _tile_scores(q_m, k_ref, i, tile, causal):
    d0 = i * tile
    s = _dot_nt(q_m, k_ref[0:d0 + tile, :])
    s_d = jnp.where(causal, s[:, d0:], NEG_INF)
    m = jnp.max(s_d, axis=-1, keepdims=True)
    if i == 0:
        return [s_d], m
    s_p = s[:, :d0]
    return [s_p, s_d], jnp.maximum(m, jnp.max(s_p, axis=-1, keepdims=True))


def _tile_output(parts, m, v_ref, i, tile):
    p = [jnp.exp2(s - m) for s in parts]
    denom = jnp.sum(p[0], axis=-1, keepdims=True)
    for extra in p[1:]:
        denom = denom + jnp.sum(extra, axis=-1, keepdims=True)
    p = [x.astype(BF16) for x in p]
    p = p[0] if len(p) == 1 else jnp.concatenate(p, axis=1)
    return _dot(p, v_ref[0:(i + 1) * tile, :]) / denom


PIPELINE_AHEAD = 2


def _pipelined(items, matmul_stage, vector_stage, ahead=PIPELINE_AHEAD):
    outs = {}
    pending = {}
    for j in range(len(items) + ahead):
        if j < len(items):
            pending[items[j]] = matmul_stage(items[j])
        if j >= ahead:
            item = items[j - ahead]
            outs[item] = vector_stage(item, pending.pop(item))
    return outs


def _causal_mask(tile):
    row = lax.broadcasted_iota(jnp.int32, (tile, tile), 0)
    col = lax.broadcasted_iota(jnp.int32, (tile, tile), 1)
    return col <= row


def _low_lanes():
    return lax.broadcasted_iota(jnp.int32, (1, LANES), 1) < LANES // 2


def _map_query(q_ref, i, tile, mp):
    q = q_ref[i * tile:(i + 1) * tile, :]
    keep = _low_lanes() if mp == 0 else jnp.logical_not(_low_lanes())
    return jnp.where(keep, q, jnp.zeros((), q.dtype))


def _diff_attn_kernel(lam_ref, q_ref, k_ref, v_ref, g_ref, o_ref, *, lam_init):
    tile = ATTN_TILE
    n_tiles = q_ref.shape[0] // tile
    causal = _causal_mask(tile)
    lp = lam_ref[...]
    lam = (jnp.exp(jnp.sum(lp[0:1] * lp[1:2], axis=-1, keepdims=True))
           - jnp.exp(jnp.sum(lp[2:3] * lp[3:4], axis=-1, keepdims=True)) + lam_init)
    gain = g_ref[...] * (1.0 - lam_init)

    def scores(item):
        i, mp = item
        return _tile_scores(_map_query(q_ref, i, tile, mp), k_ref, i, tile, causal)

    def output(item, scored):
        return _tile_output(*scored, v_ref, item[0], tile)

    items = [(i, mp) for i in reversed(range(n_tiles)) for mp in range(2)]
    outs = _pipelined(items, scores, output)
    for i in range(n_tiles):
        o = outs[(i, 0)] - lam * outs[(i, 1)]
        ms = jnp.mean(o * o, axis=-1, keepdims=True)
        o_ref[i * tile:(i + 1) * tile, :] = (o * lax.rsqrt(ms + LN_EPS) * gain).astype(BF16)


def _diff_attn(qkv, lam_params, subln_g, lam_init, layer):
    bsz, seq, _ = qkv.shape
    head = lambda off: pl.BlockSpec((None, seq, LANES), lambda b, h: (b, 0, off + h))
    return pl.pallas_call(
        functools.partial(_diff_attn_kernel, lam_init=lam_init),
        grid=(bsz, A_HEADS),
        in_specs=[_resident((4, A_DH), layer), head(0), head(A_HEADS), head(2 * A_HEADS),
                  _resident((1, A_DV), layer)],
        out_specs=head(0),
        out_shape=jax.ShapeDtypeStruct((bsz, seq, A_HEADS * A_DV), BF16),
        compiler_params=_params("parallel", "parallel"),
        name="diff_attn",
    )(lam_params, qkv, qkv, qkv, subln_g)


MASKED_SCORE = -1e30


def _moba_penalties(gate, blk):
    n_blocks, seq = gate.shape
    block = lax.broadcasted_iota(jnp.int32, (n_blocks, seq), 0)
    is_past = (block + 1) * blk <= lax.broadcasted_iota(jnp.int32, (n_blocks, seq), 1)
    pen = jnp.zeros((n_blocks, seq), F32)
    for n in range(n_blocks - 1):
        g_n = gate[n:n + 1, :]
        ahead = jnp.where(block < n, jnp.where(gate >= g_n, 1.0, 0.0),
                          jnp.where(gate > g_n, 1.0, 0.0))
        rank = jnp.sum(jnp.where(is_past, ahead, 0.0), axis=0, keepdims=True)
        pen_n = jnp.where(rank < C_TOPK, 0.0, MASKED_SCORE)
        pen = jnp.where((block == n) & is_past, pen_n, pen)
    return pen


def _moba_kernel(q_ref, k_ref, v_ref, o_ref, kaug_ref, qaug_ref):
    blk = C_BLOCK
    seq = q_ref.shape[0]
    n_blocks = seq // blk
    causal = _causal_mask(blk)
    low = _low_lanes()

    q = q_ref[...]
    key = lax.broadcasted_iota(jnp.int32, (n_blocks, seq), 1)
    first_key = lax.broadcasted_iota(jnp.int32, (n_blocks, seq), 0) * blk
    in_block = (key >= first_key) & (key < first_key + blk)
    kbar = _dot(jnp.where(in_block, 1.0 / blk, 0.0).astype(BF16), k_ref[...]).astype(BF16)
    lane = lax.broadcasted_iota(jnp.int32, (1, LANES), 1)
    blk_row = lax.broadcasted_iota(jnp.int32, (n_blocks, LANES), 0)
    blk_lane = lax.broadcasted_iota(jnp.int32, (n_blocks, LANES), 1)
    zero = jnp.zeros((), BF16)
    for hd in range(2):
        own = low if hd == 0 else jnp.logical_not(low)
        lane0 = LANES // 2 if hd == 0 else 0
        for b in range(n_blocks):
            block_id = jnp.where(lane == lane0 + b, 1.0, 0.0).astype(BF16)
            rows = slice(b * blk, (b + 1) * blk)
            kaug_ref[hd, rows, :] = jnp.where(own, k_ref[rows, :], block_id)
        q_m = jnp.where(own, q, zero)
        pen = _moba_penalties(_dot_nt(kbar, q_m), blk).astype(BF16)
        place = jnp.where(blk_lane - lane0 == blk_row, 1.0, 0.0).astype(BF16)
        placed = lax.dot_general(pen, place, (((0,), (0,)), ((), ())),
                                 preferred_element_type=F32)
        qaug_ref[hd] = q_m + placed.astype(BF16)

    def scores(item):
        i, hd = item
        return _tile_scores(qaug_ref[hd, i * blk:(i + 1) * blk, :], kaug_ref.at[hd], i, blk, causal)

    def output(item, scored):
        return _tile_output(*scored, v_ref, item[0], blk)

    items = [(i, hd) for i in reversed(range(n_blocks)) for hd in range(2)]
    outs = _pipelined(items, scores, output)
    for i in range(n_blocks):
        o_ref[i * blk:(i + 1) * blk, :] = jnp.where(low, outs[(i, 0)], outs[(i, 1)]).astype(BF16)


def _moba(qkv):
    bsz, seq, _ = qkv.shape
    pairs = C_HEADS * C_DH // LANES
    head = lambda off: pl.BlockSpec((None, seq, LANES), lambda b, h: (b, 0, off + h))
    return pl.pallas_call(
        _moba_kernel,
        grid=(bsz, pairs),
        in_specs=[head(0), head(pairs), head(2 * pairs)],
        out_specs=head(0),
        out_shape=jax.ShapeDtypeStruct((bsz, seq, C_HEADS * C_DH), BF16),
        scratch_shapes=[pltpu.VMEM((2, seq, LANES), BF16), pltpu.VMEM((2, seq, LANES), BF16)],
        compiler_params=_params("parallel", "parallel"),
        name="moba_attn",
    )(qkv, qkv, qkv)


def _mem_kv_kernel(mem_ref, wk_ref, wv_ref, k_ref, v_ref):
    mb = mem_ref[...].astype(BF16)
    k_ref[...] = _dot(mb, wk_ref[...]).astype(BF16)
    v_ref[...] = _dot(mb, wv_ref[...]).astype(BF16)


def _mem_kv(mem2d, wk, wv):
    rows = mem2d.shape[0]
    w = pl.BlockSpec((None, D_MODEL, D_MODEL), lambda l, i: (l, 0, 0))
    out = pl.BlockSpec((None, ROW_TILE, D_MODEL), lambda l, i: (l, i, 0))
    shape = jax.ShapeDtypeStruct((DEPTH, rows, D_MODEL), BF16)
    return pl.pallas_call(
        _mem_kv_kernel,
        grid=(DEPTH, rows // ROW_TILE),
        in_specs=[pl.BlockSpec((ROW_TILE, D_MODEL), lambda l, i: (i, 0)), w, w],
        out_specs=[out, out],
        out_shape=[shape, shape],
        compiler_params=_params("parallel", "parallel"),
        name="mem_kv_proj",
    )(mem2d, wk, wv)


XATTN_PARTS = D_MODEL // MXU_WIDTH


def _mixout_xattn_kernel(x_ref, oa_ref, ob_ref, oc_ref, wout_ref, g2_ref, b2_ref,
                         k_ref, v_ref, wq_ref, wo_ref, g_ref, b_ref, o_ref,
                         res_ref, q_ref, cat_ref, y_ref):
    na = A_HEADS * A_DV
    col_blocks = [slice(c * MXU_WIDTH, (c + 1) * MXU_WIDTH) for c in range(XATTN_PARTS)]

    def sub_tile(rows):
        chunk = (rows.stop - rows.start) // XATTN_PARTS
        row_chunks = [slice(rows.start + j * chunk, rows.start + (j + 1) * chunk)
                      for j in range(XATTN_PARTS)]
        for cols in col_blocks:
            mix = (_dot(oa_ref[rows, :], wout_ref[:na, cols])
                   + _dot(ob_ref[rows, :], wout_ref[na:na + B_WIDTH, cols])
                   + _dot(oc_ref[rows, :], wout_ref[na + B_WIDTH:, cols]))
            res_ref[rows, cols] = DEEPNORM_ALPHA * x_ref[rows, cols] + mix
            yield
        for rr in row_chunks:
            res_ref[rr, :] = _layer_norm(res_ref[rr, :], g2_ref[...], b2_ref[...])
            yield
        xb = res_ref[rows, :].astype(BF16)
        for cols in col_blocks:
            q_ref[rows, cols] = (_dot(xb, wq_ref[:, cols]) * (XA_DH ** -0.5 * LOG2_E)).astype(BF16)
            yield
        for h in range(XA_HEADS):
            sl = slice(h * XA_DH, (h + 1) * XA_DH)
            s = _dot_nt(q_ref[rows, sl], k_ref[:, sl])
            p = jnp.exp2(s - jnp.max(s, axis=-1, keepdims=True))
            denom = jnp.sum(p, axis=-1, keepdims=True)
            cat_ref[rows, sl] = (_dot(p.astype(BF16), v_ref[:, sl]) / denom).astype(BF16)
            yield
        attended = cat_ref[rows, :]
        for cols in col_blocks:
            y_ref[rows, cols] = _dot(attended, wo_ref[:, cols])
            yield
        for j, rr in enumerate(row_chunks):
            o_ref[rr, :] = _layer_norm(DEEPNORM_ALPHA * res_ref[rr, :] + y_ref[rr, :],
                                       g_ref[...], b_ref[...])
            if j + 1 < XATTN_PARTS:
                yield

    _interleave_sub_tiles(sub_tile, skew=XATTN_PARTS)


def _mixout_xattn(x3, oa, ob, oc, w_out, g2, b2, k, v, wq, wo, g3, b3, layer):
    bsz, seq, _ = x3.shape
    rows = lambda n: pl.BlockSpec((None, ROW_TILE, n), lambda b_, i: (b_, i, 0))
    kv = pl.BlockSpec((None, None, MEM_LEN, D_MODEL), lambda b_, i: (layer, b_, 0, 0))
    weight = _resident((D_MODEL, D_MODEL), layer)
    vec = _resident((1, D_MODEL), layer)
    return pl.pallas_call(
        _mixout_xattn_kernel,
        grid=(bsz, seq // ROW_TILE),
        in_specs=[rows(D_MODEL), rows(A_HEADS * A_DV), rows(B_WIDTH), rows(C_HEADS * C_DH),
                  weight, vec, vec, kv, kv, weight, weight, vec, vec],
        out_specs=rows(D_MODEL),
        out_shape=jax.ShapeDtypeStruct((bsz, seq, D_MODEL), F32),
        scratch_shapes=[pltpu.VMEM((ROW_TILE, D_MODEL), F32), pltpu.VMEM((ROW_TILE, D_MODEL), BF16),
                        pltpu.VMEM((ROW_TILE, D_MODEL), BF16), pltpu.VMEM((ROW_TILE, D_MODEL), F32)],
        compiler_params=_params("parallel", "parallel"),
        name="mixout_xattn_ln",
    )(x3, oa, ob, oc, w_out, g2, b2, k, v, wq, wo, g3, b3)


def kernel(x, mem, positions, ffn1_w_gate, ffn1_w_up, ffn1_w_down, ln1_g, ln1_b, mix_w_in, diff_lq1, diff_lk1, diff_lq2, diff_lk2, diff_subln_g, sgu_ln_g, sgu_ln_b, sgu_w, sgu_b, mix_w_out, ln2_g, ln2_b, xa_wq, xa_wk, xa_wv, xa_wo, ln3_g, ln3_b, ffn2_w_gate, ffn2_w_up, ffn2_w_down, ln4_g, ln4_b):
    bsz, seq, d = x.shape
    t = bsz * seq
    bf = lambda w: w.astype(BF16)
    vec = lambda p: p.reshape(DEPTH, 1, -1)

    cos_t, sin_t = _rope_tables(positions)
    mem_k, mem_v = _mem_kv(mem.reshape(bsz * MEM_LEN, d), bf(xa_wk), bf(xa_wv))
    mem_k = mem_k.reshape(DEPTH, bsz, MEM_LEN, d)
    mem_v = mem_v.reshape(DEPTH, bsz, MEM_LEN, d)
    sgu_bias = jnp.repeat(jnp.swapaxes(sgu_b, 1, 2), B_DG, axis=2)
    lam_params = jnp.stack([diff_lq1, diff_lk1, diff_lq2, diff_lk2], axis=1)
    ffn1 = (bf(ffn1_w_gate), bf(ffn1_w_up), bf(ffn1_w_down), vec(ln1_g), vec(ln1_b))
    ffn2 = (bf(ffn2_w_gate), bf(ffn2_w_up), bf(ffn2_w_down), vec(ln4_g), vec(ln4_b))
    w_in, w_out, wq, wo = bf(mix_w_in), bf(mix_w_out), bf(xa_wq), bf(xa_wo)

    h = x.reshape(t, d)
    for l in range(DEPTH):
        lam_init = 0.8 - 0.6 * math.exp(-0.3 * l)
        h = _ffn_ln(h, *ffn1, l)
        qkv_a, ob, qkv_c = _inproj(h, w_in, cos_t, sin_t, vec(sgu_ln_g), vec(sgu_ln_b),
                                   sgu_w, sgu_bias, l)
        oa = _diff_attn(qkv_a.reshape(bsz, seq, A_COLS), lam_params, vec(diff_subln_g), lam_init, l)
        oc = _moba(qkv_c.reshape(bsz, seq, C_COLS))
        h = _mixout_xattn(h.reshape(bsz, seq, d), oa, ob.reshape(bsz, seq, B_WIDTH), oc,
                          w_out, vec(ln2_g), vec(ln2_b), mem_k, mem_v, wq, wo,
                          vec(ln3_g), vec(ln3_b), l).reshape(t, d)
        h = _ffn_ln(h, *ffn2, l)
    return h.reshape(bsz, seq, d)
```

```python
import functools
import math

import jax
import jax.numpy as jnp
from jax import lax
from jax.experimental import pallas as pl
from jax.experimental.pallas import tpu as pltpu

D_MODEL = 1024
DEPTH = 2
MEM_LEN = 256
A_HEADS = 4
A_DH = 64
A_DV = 2 * A_DH
B_GROUPS = 4
B_DG = 64
B_WIDTH = B_GROUPS * B_DG
B_CHUNK = 128
C_HEADS = 4
C_DH = 64
C_BLOCK = 256
C_TOPK = 3
A_COLS = 3 * A_HEADS * A_DV
C_COLS = 3 * C_HEADS * C_DH
IN_COLS = A_COLS + 2 * B_WIDTH + C_COLS
XA_HEADS = 4
XA_DH = D_MODEL // XA_HEADS
D_FF = 2816
ROPE_THETA = 10000.0
LN_EPS = 1e-5
DEEPNORM_ALPHA = (2.0 * DEPTH) ** 0.25

LANES = 128
MXU_WIDTH = 256
VMEM_LIMIT_BYTES = 56 * 1024 * 1024

ROW_TILE = 1024
FFN_ROW_TILE = 2048
SUB_TILE = 512
FF_CHUNK = MXU_WIDTH
ATTN_TILE = 256

F32 = jnp.float32
BF16 = jnp.bfloat16
NEG_INF = float("-inf")
LOG2_E = math.log2(math.e)


def _params(*sem):
    return pltpu.CompilerParams(dimension_semantics=sem, vmem_limit_bytes=VMEM_LIMIT_BYTES)


def _resident(shape, layer):
    index = (layer,) + (0,) * len(shape)
    return pl.BlockSpec((None,) + tuple(shape), lambda *_: index, pipeline_mode=pl.Buffered(1))


def _layer_norm(y, g, b, eps=LN_EPS):
    mu = jnp.mean(y, axis=-1, keepdims=True)
    d = y - mu
    var = jnp.mean(d * d, axis=-1, keepdims=True)
    return d * lax.rsqrt(var + eps) * g + b


def _interleave_sub_tiles(stages, sub_tile=None, skew=1, row_tile=None):
    sub_tile = sub_tile or SUB_TILE
    row_tile = row_tile or ROW_TILE
    gens = [stages(slice(s * sub_tile, (s + 1) * sub_tile)) for s in range(row_tile // sub_tile)]
    live = list(range(len(gens)))
    step = 0
    while live:
        for s in list(live):
            if step >= s * skew:
                try:
                    next(gens[s])
                except StopIteration:
                    live.remove(s)
        step += 1


def _dot(a, b):
    return jnp.dot(a, b, preferred_element_type=F32)


def _dot_nt(a, b):
    return lax.dot_general(a, b, (((1,), (1,)), ((), ())), preferred_element_type=F32)


ROPE_PACK = LANES // (A_DH // 2)


def _rope_kernel(pos_ref, invf_ref, cos_ref, sin_ref):
    half = A_DH // 2
    ang = pos_ref[...] * invf_ref[...]
    lane = lax.broadcasted_iota(jnp.int32, (1, LANES), 1)
    group = lane // half
    sign = jnp.where(lane % A_DH < half, -1.0, 1.0).astype(F32)
    packed_rows = pos_ref.shape[0]
    for table, out_ref, scale in ((jnp.cos(ang), cos_ref, None), (jnp.sin(ang), sin_ref, sign)):
        rolled = [table] + [pltpu.roll(table, half * k, 1) for k in range(1, ROPE_PACK)]
        for j in range(ROPE_PACK):
            spread = rolled[(0 - j) % ROPE_PACK]
            for g in range(1, ROPE_PACK):
                spread = jnp.where(group == g, rolled[(g - j) % ROPE_PACK], spread)
            if scale is not None:
                spread = spread * scale
            out_ref[pl.ds(j, packed_rows, stride=ROPE_PACK), :] = spread


def _rope_tables(positions):
    t = positions.size
    half = A_DH // 2
    inv_freq = 1.0 / (ROPE_THETA ** (jnp.arange(0, A_DH, 2, dtype=F32) / A_DH))
    invf = jnp.tile(inv_freq, ROPE_PACK).reshape(1, LANES)
    pos = jnp.repeat(positions.astype(F32).reshape(t // ROPE_PACK, ROPE_PACK), half, axis=1)
    tm = 2048
    out = jax.ShapeDtypeStruct((t, LANES), F32)
    return pl.pallas_call(
        _rope_kernel,
        grid=(t // tm,),
        in_specs=[pl.BlockSpec((tm // ROPE_PACK, LANES), lambda i: (i, 0)),
                  pl.BlockSpec((1, LANES), lambda i: (0, 0))],
        out_specs=[pl.BlockSpec((tm, LANES), lambda i: (i, 0))] * 2,
        out_shape=[out, out],
        compiler_params=_params("parallel"),
        name="rope_tables",
    )(pos, invf)


def _ffn_ln_kernel(x_ref, wg_ref, wu_ref, wd_ref, g_ref, b_ref, o_ref):
    acc_ref = o_ref
    def sub_tile(rows):
        xb = x_ref[rows, :].astype(BF16)
        for c in range(D_FF // FF_CHUNK):
            cols = slice(c * FF_CHUNK, (c + 1) * FF_CHUNK)
            gate = _dot(xb, wg_ref[:, cols])
            up = _dot(xb, wu_ref[:, cols])
            act = (gate * jax.nn.sigmoid(gate) * up).astype(BF16)
            part = _dot(act, wd_ref[cols, :])
            if c == 0:
                acc_ref[rows, :] = (2.0 * DEEPNORM_ALPHA) * x_ref[rows, :] + part
            else:
                acc_ref[rows, :] += part
            yield
        o_ref[rows, :] = _layer_norm(acc_ref[rows, :], g_ref[...], b_ref[...], eps=4.0 * LN_EPS)

    _interleave_sub_tiles(sub_tile, row_tile=FFN_ROW_TILE)


def _ffn_ln(x, wg, wu, wd, g, b, layer):
    t = x.shape[0]
    row = pl.BlockSpec((FFN_ROW_TILE, D_MODEL), lambda i: (i, 0))
    return pl.pallas_call(
        _ffn_ln_kernel,
        grid=(t // FFN_ROW_TILE,),
        in_specs=[row, _resident((D_MODEL, D_FF), layer), _resident((D_MODEL, D_FF), layer),
                  _resident((D_FF, D_MODEL), layer), _resident((1, D_MODEL), layer),
                  _resident((1, D_MODEL), layer)],
        out_specs=row,
        out_shape=jax.ShapeDtypeStruct((t, D_MODEL), F32),
        compiler_params=_params("parallel"),
        name="ffn_ln",
    )(x, wg, wu, wd, g, b)


def _gelu(y):
    return 0.5 * y * (1.0 + lax.erf(y * (2.0 ** -0.5)))


def _inproj_kernel(x_ref, w_ref, cos_ref, sin_ref, lng_ref, lnb_ref, sw_ref, sb_ref,
                   a_ref, ob_ref, c_ref, u_ref):
    lane = lax.broadcasted_iota(jnp.int32, (1, LANES), 1)
    first_half = lane % A_DH < A_DH // 2
    low_lanes = lane < B_DG
    tri_r = lax.broadcasted_iota(jnp.int32, (B_CHUNK, B_CHUNK), 0)
    tri_c = lax.broadcasted_iota(jnp.int32, (B_CHUNK, B_CHUNK), 1)
    w_tril = [jnp.where(tri_r >= tri_c, sw_ref[g], 0.0).astype(BF16) for g in range(B_GROUPS)]
    zero = jnp.zeros((), BF16)
    q_scale = A_DH ** -0.5 * LOG2_E

    xb = x_ref[...].astype(BF16)

    def project(col):
        return _dot(xb, w_ref[:, col:col + MXU_WIDTH])

    def rope_to(out_ref, out_col, scale):
        def epilogue(_, y):
            for hb in range(MXU_WIDTH // LANES):
                yh = y[:, hb * LANES:(hb + 1) * LANES]
                swapped = jnp.where(first_half, pltpu.roll(yh, LANES - A_DH // 2, 1),
                                    pltpu.roll(yh, A_DH // 2, 1))
                roped = (yh * cos_ref[...] + swapped * sin_ref[...]) * scale
                out_ref[:, out_col + hb * LANES:out_col + (hb + 1) * LANES] = roped.astype(BF16)
        return epilogue

    def plain_to(out_ref, out_col):
        def epilogue(_, y):
            out_ref[:, out_col:out_col + MXU_WIDTH] = y.astype(BF16)
        return epilogue

    def gate_input(_, y):
        u_ref[...] = _gelu(y)

    def spatial_gating(_, y):
        vn = _layer_norm(_gelu(y), lng_ref[...], lnb_ref[...]).astype(BF16)
        for ci in range(ROW_TILE // B_CHUNK):
            rsl = slice(ci * B_CHUNK, (ci + 1) * B_CHUNK)
            for hb in range(B_WIDTH // LANES):
                csl = slice(hb * LANES, (hb + 1) * LANES)
                vblk = vn[rsl, csl]
                mix = (_dot(w_tril[2 * hb], jnp.where(low_lanes, vblk, zero))
                       + _dot(w_tril[2 * hb + 1], jnp.where(low_lanes, zero, vblk))
                       + sb_ref[:, csl])
                ob_ref[rsl, csl] = (u_ref[rsl, csl] * mix).astype(BF16)

    c0 = A_COLS + 2 * B_WIDTH
    blocks = [(A_COLS, gate_input), (A_COLS + B_WIDTH, spatial_gating)]
    for j in range(2 * A_HEADS * A_DV // MXU_WIDTH):
        col = j * MXU_WIDTH
        blocks.append((col, rope_to(a_ref, col, q_scale if col < A_HEADS * A_DV else 1.0)))
    for j in range(A_HEADS * A_DV // MXU_WIDTH):
        col = 2 * A_HEADS * A_DV + j * MXU_WIDTH
        blocks.append((col, plain_to(a_ref, col)))
    blocks.append((c0, rope_to(c_ref, 0, C_DH ** -0.5 * LOG2_E)))
    blocks.append((c0 + MXU_WIDTH, rope_to(c_ref, MXU_WIDTH, 1.0)))
    blocks.append((c0 + 2 * MXU_WIDTH, plain_to(c_ref, 2 * MXU_WIDTH)))
    epilogues = dict(blocks)
    _pipelined([col for col, _ in blocks], project, lambda col, y: epilogues[col](col, y), ahead=1)


def _inproj(x, w_in, cos_t, sin_t, ln_g, ln_b, sgu_w, sgu_bias, layer):
    t = x.shape[0]
    rows = lambda n: pl.BlockSpec((ROW_TILE, n), lambda i: (i, 0))
    return pl.pallas_call(
        _inproj_kernel,
        grid=(t // ROW_TILE,),
        in_specs=[rows(D_MODEL), _resident((D_MODEL, IN_COLS), layer), rows(LANES), rows(LANES),
                  _resident((1, B_WIDTH), layer), _resident((1, B_WIDTH), layer),
                  _resident((B_GROUPS, B_CHUNK, B_CHUNK), layer),
                  _resident((B_CHUNK, B_WIDTH), layer)],
        out_specs=[rows(A_COLS), rows(B_WIDTH), rows(C_COLS)],
        out_shape=[jax.ShapeDtypeStruct((t, A_COLS), BF16),
                   jax.ShapeDtypeStruct((t, B_WIDTH), BF16),
                   jax.ShapeDtypeStruct((t, C_COLS), BF16)],
        scratch_shapes=[pltpu.VMEM((ROW_TILE, B_WIDTH), F32)],
        compiler_params=_params("parallel"),
        name="mixer_inproj",
    )(x, w_in, cos_t, sin_t, ln_g, ln_b, sgu_w, sgu_bias)


def _tile_scores(q_m, k_ref, i, tile, causal):
    d0 = i * tile
    s = _dot_nt(q_m, k_ref[0:d0 + tile, :])
    s_d = jnp.where(causal, s[:, d0:], NEG_INF)
    m = jnp.max(s_d, axis=-1, keepdims=True)
    if i == 0:
        return [s_d], m
    s_p = s[:, :d0]
    return [s_p, s_d], jnp.maximum(m, jnp.max(s_p, axis=-1, keepdims=True))


def _tile_output(parts, m, v_ref, i, tile):
    p = [jnp.exp2(s - m) for s in parts]
    denom = jnp.sum(p[0], axis=-1, keepdims=True)
    for extra in p[1:]:
        denom = denom + jnp.sum(extra, axis=-1, keepdims=True)
    p = [x.astype(BF16) for x in p]
    p = p[0] if len(p) == 1 else jnp.concatenate(p, axis=1)
    return _dot(p, v_ref[0:(i + 1) * tile, :]) / denom


PIPELINE_AHEAD = 2


def _pipelined(items, matmul_stage, vector_stage, ahead=PIPELINE_AHEAD):
    outs = {}
    pending = {}
    for j in range(len(items) + ahead):
        if j < len(items):
            pending[items[j]] = matmul_stage(items[j])
        if j >= ahead:
            item = items[j - ahead]
            outs[item] = vector_stage(item, pending.pop(item))
    return outs


def _causal_mask(tile):
    row = lax.broadcasted_iota(jnp.int32, (tile, tile), 0)
    col = lax.broadcasted_iota(jnp.int32, (tile, tile), 1)
    return col <= row


def _low_lanes():
    return lax.broadcasted_iota(jnp.int32, (1, LANES), 1) < LANES // 2


def _map_query(q_ref, i, tile, mp):
    q = q_ref[i * tile:(i + 1) * tile, :]
    keep = _low_lanes() if mp == 0 else jnp.logical_not(_low_lanes())
    return jnp.where(keep, q, jnp.zeros((), q.dtype))


def _diff_attn_kernel(lam_ref, q_ref, k_ref, v_ref, g_ref, o_ref, *, lam_init):
    tile = ATTN_TILE
    n_tiles = q_ref.shape[0] // tile
    causal = _causal_mask(tile)
    lp = lam_ref[...]
    lam = (jnp.exp(jnp.sum(lp[0:1] * lp[1:2], axis=-1, keepdims=True))
           - jnp.exp(jnp.sum(lp[2:3] * lp[3:4], axis=-1, keepdims=True)) + lam_init)
    gain = g_ref[...] * (1.0 - lam_init)

    def scores(item):
        i, mp = item
        return _tile_scores(_map_query(q_ref, i, tile, mp), k_ref, i, tile, causal)

    def output(item, scored):
        return _tile_output(*scored, v_ref, item[0], tile)

    items = [(i, mp) for i in reversed(range(n_tiles)) for mp in range(2)]
    outs = _pipelined(items, scores, output)
    for i in range(n_tiles):
        o = outs[(i, 0)] - lam * outs[(i, 1)]
        ms = jnp.mean(o * o, axis=-1, keepdims=True)
        o_ref[i * tile:(i + 1) * tile, :] = (o * lax.rsqrt(ms + LN_EPS) * gain).astype(BF16)


def _diff_attn(qkv, lam_params, subln_g, lam_init, layer):
    bsz, seq, _ = qkv.shape
    head = lambda off: pl.BlockSpec((None, seq, LANES), lambda b, h: (b, 0, off + h))
    return pl.pallas_call(
        functools.partial(_diff_attn_kernel, lam_init=lam_init),
        grid=(bsz, A_HEADS),
        in_specs=[_resident((4, A_DH), layer), head(0), head(A_HEADS), head(2 * A_HEADS),
                  _resident((1, A_DV), layer)],
        out_specs=head(0),
        out_shape=jax.ShapeDtypeStruct((bsz, seq, A_HEADS * A_DV), BF16),
        compiler_params=_params("parallel", "parallel"),
        name="diff_attn",
    )(lam_params, qkv, qkv, qkv, subln_g)


MASKED_SCORE = -1e30


def _moba_penalties(gate, blk):
    n_blocks, seq = gate.shape
    block = lax.broadcasted_iota(jnp.int32, (n_blocks, seq), 0)
    is_past = (block + 1) * blk <= lax.broadcasted_iota(jnp.int32, (n_blocks, seq), 1)
    pen = jnp.zeros((n_blocks, seq), F32)
    for n in range(n_blocks - 1):
        g_n = gate[n:n + 1, :]
        ahead = jnp.where(block < n, jnp.where(gate >= g_n, 1.0, 0.0),
                          jnp.where(gate > g_n, 1.0, 0.0))
        rank = jnp.sum(jnp.where(is_past, ahead, 0.0), axis=0, keepdims=True)
        pen_n = jnp.where(rank < C_TOPK, 0.0, MASKED_SCORE)
        pen = jnp.where((block == n) & is_past, pen_n, pen)
    return pen


def _moba_kernel(q_ref, k_ref, v_ref, o_ref, kaug_ref, qaug_ref):
    blk = C_BLOCK
    seq = q_ref.shape[0]
    n_blocks = seq // blk
    causal = _causal_mask(blk)
    low = _low_lanes()

    q = q_ref[...]
    key = lax.broadcasted_iota(jnp.int32, (n_blocks, seq), 1)
    first_key = lax.broadcasted_iota(jnp.int32, (n_blocks, seq), 0) * blk
    in_block = (key >= first_key) & (key < first_key + blk)
    kbar = _dot(jnp.where(in_block, 1.0 / blk, 0.0).astype(BF16), k_ref[...]).astype(BF16)
    lane = lax.broadcasted_iota(jnp.int32, (1, LANES), 1)
    blk_row = lax.broadcasted_iota(jnp.int32, (n_blocks, LANES), 0)
    blk_lane = lax.broadcasted_iota(jnp.int32, (n_blocks, LANES), 1)
    zero = jnp.zeros((), BF16)
    for hd in range(2):
        own = low if hd == 0 else jnp.logical_not(low)
        lane0 = LANES // 2 if hd == 0 else 0
        for b in range(n_blocks):
            block_id = jnp.where(lane == lane0 + b, 1.0, 0.0).astype(BF16)
            rows = slice(b * blk, (b + 1) * blk)
            kaug_ref[hd, rows, :] = jnp.where(own, k_ref[rows, :], block_id)
        q_m = jnp.where(own, q, zero)
        pen = _moba_penalties(_dot_nt(kbar, q_m), blk).astype(BF16)
        place = jnp.where(blk_lane - lane0 == blk_row, 1.0, 0.0).astype(BF16)
        placed = lax.dot_general(pen, place, (((0,), (0,)), ((), ())),
                                 preferred_element_type=F32)
        qaug_ref[hd] = q_m + placed.astype(BF16)

    def scores(item):
        i, hd = item
        return _tile_scores(qaug_ref[hd, i * blk:(i + 1) * blk, :], kaug_ref.at[hd], i, blk, causal)

    def output(item, scored):
        return _tile_output(*scored, v_ref, item[0], blk)

    items = [(i, hd) for i in reversed(range(n_blocks)) for hd in range(2)]
    outs = _pipelined(items, scores, output)
    for i in range(n_blocks):
        o_ref[i * blk:(i + 1) * blk, :] = jnp.where(low, outs[(i, 0)], outs[(i, 1)]).astype(BF16)


def _moba(qkv):
    bsz, seq, _ = qkv.shape
    pairs = C_HEADS * C_DH // LANES
    head = lambda off: pl.BlockSpec((None, seq, LANES), lambda b, h: (b, 0, off + h))
    return pl.pallas_call(
        _moba_kernel,
        grid=(bsz, pairs),
        in_specs=[head(0), head(pairs), head(2 * pairs)],
        out_specs=head(0),
        out_shape=jax.ShapeDtypeStruct((bsz, seq, C_HEADS * C_DH), BF16),
        scratch_shapes=[pltpu.VMEM((2, seq, LANES), BF16), pltpu.VMEM((2, seq, LANES), BF16)],
        compiler_params=_params("parallel", "parallel"),
        name="moba_attn",
    )(qkv, qkv, qkv)


def _mem_kv_kernel(mem_ref, wk_ref, wv_ref, k_ref, v_ref):
    mb = mem_ref[...].astype(BF16)
    k_ref[...] = _dot(mb, wk_ref[...]).astype(BF16)
    v_ref[...] = _dot(mb, wv_ref[...]).astype(BF16)


def _mem_kv(mem2d, wk, wv):
    rows = mem2d.shape[0]
    w = pl.BlockSpec((None, D_MODEL, D_MODEL), lambda l, i: (l, 0, 0))
    out = pl.BlockSpec((None, ROW_TILE, D_MODEL), lambda l, i: (l, i, 0))
    shape = jax.ShapeDtypeStruct((DEPTH, rows, D_MODEL), BF16)
    return pl.pallas_call(
        _mem_kv_kernel,
        grid=(DEPTH, rows // ROW_TILE),
        in_specs=[pl.BlockSpec((ROW_TILE, D_MODEL), lambda l, i: (i, 0)), w, w],
        out_specs=[out, out],
        out_shape=[shape, shape],
        compiler_params=_params("parallel", "parallel"),
        name="mem_kv_proj",
    )(mem2d, wk, wv)


XATTN_PARTS = D_MODEL // MXU_WIDTH


def _mixout_xattn_kernel(x_ref, oa_ref, ob_ref, oc_ref, wout_ref, g2_ref, b2_ref,
                         k_ref, v_ref, wq_ref, wo_ref, g_ref, b_ref, o_ref,
                         res_ref, q_ref, cat_ref, y_ref):
    na = A_HEADS * A_DV
    col_blocks = [slice(c * MXU_WIDTH, (c + 1) * MXU_WIDTH) for c in range(XATTN_PARTS)]

    def sub_tile(rows):
        chunk = (rows.stop - rows.start) // XATTN_PARTS
        row_chunks = [slice(rows.start + j * chunk, rows.start + (j + 1) * chunk)
                      for j in range(XATTN_PARTS)]
        for cols in col_blocks:
            mix = (_dot(oa_ref[rows, :], wout_ref[:na, cols])
                   + _dot(ob_ref[rows, :], wout_ref[na:na + B_WIDTH, cols])
                   + _dot(oc_ref[rows, :], wout_ref[na + B_WIDTH:, cols]))
            res_ref[rows, cols] = DEEPNORM_ALPHA * x_ref[rows, cols] + mix
            yield
        for rr in row_chunks:
            res_ref[rr, :] = _layer_norm(res_ref[rr, :], g2_ref[...], b2_ref[...])
            yield
        xb = res_ref[rows, :].astype(BF16)
        for cols in col_blocks:
            q_ref[rows, cols] = (_dot(xb, wq_ref[:, cols]) * (XA_DH ** -0.5 * LOG2_E)).astype(BF16)
            yield
        for h in range(XA_HEADS):
            sl = slice(h * XA_DH, (h + 1) * XA_DH)
            s = _dot_nt(q_ref[rows, sl], k_ref[:, sl])
            p = jnp.exp2(s - jnp.max(s, axis=-1, keepdims=True))
            denom = jnp.sum(p, axis=-1, keepdims=True)
            cat_ref[rows, sl] = (_dot(p.astype(BF16), v_ref[:, sl]) / denom).astype(BF16)
            yield
        attended = cat_ref[rows, :]
        for cols in col_blocks:
            y_ref[rows, cols] = _dot(attended, wo_ref[:, cols])
            yield
        for j, rr in enumerate(row_chunks):
            o_ref[rr, :] = _layer_norm(DEEPNORM_ALPHA * res_ref[rr, :] + y_ref[rr, :],
                                       g_ref[...], b_ref[...])
            if j + 1 < XATTN_PARTS:
                yield

    _interleave_sub_tiles(sub_tile, skew=XATTN_PARTS)


def _mixout_xattn(x3, oa, ob, oc, w_out, g2, b2, k, v, wq, wo, g3, b3, layer):
    bsz, seq, _ = x3.shape
    rows = lambda n: pl.BlockSpec((None, ROW_TILE, n), lambda b_, i: (b_, i, 0))
    kv = pl.BlockSpec((None, None, MEM_LEN, D_MODEL), lambda b_, i: (layer, b_, 0, 0))
    weight = _resident((D_MODEL, D_MODEL), layer)
    vec = _resident((1, D_MODEL), layer)
    return pl.pallas_call(
        _mixout_xattn_kernel,
        grid=(bsz, seq // ROW_TILE),
        in_specs=[rows(D_MODEL), rows(A_HEADS * A_DV), rows(B_WIDTH), rows(C_HEADS * C_DH),
                  weight, vec, vec, kv, kv, weight, weight, vec, vec],
        out_specs=rows(D_MODEL),
        out_shape=jax.ShapeDtypeStruct((bsz, seq, D_MODEL), F32),
        scratch_shapes=[pltpu.VMEM((ROW_TILE, D_MODEL), F32), pltpu.VMEM((ROW_TILE, D_MODEL), BF16),
                        pltpu.VMEM((ROW_TILE, D_MODEL), BF16), pltpu.VMEM((ROW_TILE, D_MODEL), F32)],
        compiler_params=_params("parallel", "parallel"),
        name="mixout_xattn_ln",
    )(x3, oa, ob, oc, w_out, g2, b2, k, v, wq, wo, g3, b3)


def kernel(x, mem, positions, ffn1_w_gate, ffn1_w_up, ffn1_w_down, ln1_g, ln1_b, mix_w_in, diff_lq1, diff_lk1, diff_lq2, diff_lk2, diff_subln_g, sgu_ln_g, sgu_ln_b, sgu_w, sgu_b, mix_w_out, ln2_g, ln2_b, xa_wq, xa_wk, xa_wv, xa_wo, ln3_g, ln3_b, ffn2_w_gate, ffn2_w_up, ffn2_w_down, ln4_g, ln4_b):
    bsz, seq, d = x.shape
    t = bsz * seq
    bf = lambda w: w.astype(BF16)
    vec = lambda p: p.reshape(DEPTH, 1, -1)

    cos_t, sin_t = _rope_tables(positions)
    mem_k, mem_v = _mem_kv(mem.reshape(bsz * MEM_LEN, d), bf(xa_wk), bf(xa_wv))
    mem_k = mem_k.reshape(DEPTH, bsz, MEM_LEN, d)
    mem_v = mem_v.reshape(DEPTH, bsz, MEM_LEN, d)
    sgu_bias = jnp.repeat(jnp.swapaxes(sgu_b, 1, 2), B_DG, axis=2)
    lam_params = jnp.stack([diff_lq1, diff_lk1, diff_lq2, diff_lk2], axis=1)
    ffn1 = (bf(ffn1_w_gate), bf(ffn1_w_up), bf(ffn1_w_down), vec(ln1_g), vec(ln1_b))
    ffn2 = (bf(ffn2_w_gate), bf(ffn2_w_up), bf(ffn2_w_down), vec(ln4_g), vec(ln4_b))
    w_in, w_out, wq, wo = bf(mix_w_in), bf(mix_w_out), bf(xa_wq), bf(xa_wo)

    h = x.reshape(t, d)
    for l in range(DEPTH):
        lam_init = 0.8 - 0.6 * math.exp(-0.3 * l)
        h = _ffn_ln(h, *ffn1, l)
        qkv_a, ob, qkv_c = _inproj(h, w_in, cos_t, sin_t, vec(sgu_ln_g), vec(sgu_ln_b),
                                   sgu_w, sgu_bias, l)
        oa = _diff_attn(qkv_a.reshape(bsz, seq, A_COLS), lam_params, vec(diff_subln_g), lam_init, l)
        oc = _moba(qkv_c.reshape(bsz, seq, C_COLS))
        h = _mixout_xattn(h.reshape(bsz, seq, d), oa, ob.reshape(bsz, seq, B_WIDTH), oc,
                          w_out, vec(ln2_g), vec(ln2_b), mem_k, mem_v, wq, wo,
                          vec(ln3_g), vec(ln3_b), l).reshape(t, d)
        h = _ffn_ln(h, *ffn2, l)
    return h.reshape(bsz, seq, d)
```

```python
import functools
import math

import jax
import jax.numpy as jnp
from jax import lax
from jax.experimental import pallas as pl
from jax.experimental.pallas import tpu as pltpu

D_MODEL = 1024
DEPTH = 2
MEM_LEN = 256
A_HEADS = 4
A_DH = 64
A_DV = 2 * A_DH
B_GROUPS = 4
B_DG = 64
B_WIDTH = B_GROUPS * B_DG
B_CHUNK = 128
C_HEADS = 4
C_DH = 64
C_BLOCK = 256
C_TOPK = 3
A_COLS = 3 * A_HEADS * A_DV
C_COLS = 3 * C_HEADS * C_DH
IN_COLS = A_COLS + 2 * B_WIDTH + C_COLS
XA_HEADS = 4
XA_DH = D_MODEL // XA_HEADS
D_FF = 2816
ROPE_THETA = 10000.0
LN_EPS = 1e-5
DEEPNORM_ALPHA = (2.0 * DEPTH) ** 0.25

LANES = 128
MXU_WIDTH = 256
VMEM_LIMIT_BYTES = 56 * 1024 * 1024

ROW_TILE = 1024
SUB_TILE = 512
FF_CHUNK = MXU_WIDTH
ATTN_TILE = 256
DIFF_HEADS_PER_STEP = 2

F32 = jnp.float32
BF16 = jnp.bfloat16
NEG_INF = float("-inf")
LOG2_E = math.log2(math.e)


def _params(*sem):
    return pltpu.CompilerParams(dimension_semantics=sem, vmem_limit_bytes=VMEM_LIMIT_BYTES)


def _resident(shape, layer):
    index = (layer,) + (0,) * len(shape)
    return pl.BlockSpec((None,) + tuple(shape), lambda *_: index, pipeline_mode=pl.Buffered(1))


def _layer_norm(y, g, b, eps=LN_EPS):
    mu = jnp.mean(y, axis=-1, keepdims=True)
    d = y - mu
    var = jnp.mean(d * d, axis=-1, keepdims=True)
    return d * lax.rsqrt(var + eps) * g + b


def _interleave_sub_tiles(stages, sub_tile=None, skew=1):
    sub_tile = sub_tile or SUB_TILE
    gens = [stages(slice(s * sub_tile, (s + 1) * sub_tile)) for s in range(ROW_TILE // sub_tile)]
    live = list(range(len(gens)))
    step = 0
    while live:
        for s in list(live):
            if step >= s * skew:
                try:
                    next(gens[s])
                except StopIteration:
                    live.remove(s)
        step += 1


def _dot(a, b):
    return jnp.dot(a, b, preferred_element_type=F32)


def _dot_nt(a, b):
    return lax.dot_general(a, b, (((1,), (1,)), ((), ())), preferred_element_type=F32)


ROPE_PACK = LANES // (A_DH // 2)


def _rope_kernel(pos_ref, invf_ref, cos_ref, sin_ref):
    half = A_DH // 2
    ang = pos_ref[...] * invf_ref[...]
    lane = lax.broadcasted_iota(jnp.int32, (1, LANES), 1)
    group = lane // half
    sign = jnp.where(lane % A_DH < half, -1.0, 1.0).astype(F32)
    packed_rows = pos_ref.shape[0]
    for table, out_ref, scale in ((jnp.cos(ang), cos_ref, None), (jnp.sin(ang), sin_ref, sign)):
        rolled = [table] + [pltpu.roll(table, half * k, 1) for k in range(1, ROPE_PACK)]
        for j in range(ROPE_PACK):
            spread = rolled[(0 - j) % ROPE_PACK]
            for g in range(1, ROPE_PACK):
                spread = jnp.where(group == g, rolled[(g - j) % ROPE_PACK], spread)
            if scale is not None:
                spread = spread * scale
            out_ref[pl.ds(j, packed_rows, stride=ROPE_PACK), :] = spread


def _rope_tables(positions):
    t = positions.size
    half = A_DH // 2
    inv_freq = 1.0 / (ROPE_THETA ** (jnp.arange(0, A_DH, 2, dtype=F32) / A_DH))
    invf = jnp.tile(inv_freq, ROPE_PACK).reshape(1, LANES)
    pos = jnp.repeat(positions.astype(F32).reshape(t // ROPE_PACK, ROPE_PACK), half, axis=1)
    tm = 2048
    out = jax.ShapeDtypeStruct((t, LANES), F32)
    return pl.pallas_call(
        _rope_kernel,
        grid=(t // tm,),
        in_specs=[pl.BlockSpec((tm // ROPE_PACK, LANES), lambda i: (i, 0)),
                  pl.BlockSpec((1, LANES), lambda i: (0, 0))],
        out_specs=[pl.BlockSpec((tm, LANES), lambda i: (i, 0))] * 2,
        out_shape=[out, out],
        compiler_params=_params("parallel"),
        name="rope_tables",
    )(pos, invf)


def _ffn_ln_kernel(x_ref, wg_ref, wu_ref, wd_ref, g_ref, b_ref, o_ref, acc_ref):
    def sub_tile(rows):
        xb = x_ref[rows, :].astype(BF16)
        for c in range(D_FF // FF_CHUNK):
            cols = slice(c * FF_CHUNK, (c + 1) * FF_CHUNK)
            gate = _dot(xb, wg_ref[:, cols])
            up = _dot(xb, wu_ref[:, cols])
            act = (gate * jax.nn.sigmoid(gate) * up).astype(BF16)
            part = _dot(act, wd_ref[cols, :])
            if c == 0:
                acc_ref[rows, :] = (2.0 * DEEPNORM_ALPHA) * x_ref[rows, :] + part
            else:
                acc_ref[rows, :] += part
            yield
        o_ref[rows, :] = _layer_norm(acc_ref[rows, :], g_ref[...], b_ref[...], eps=4.0 * LN_EPS)

    _interleave_sub_tiles(sub_tile)


def _ffn_ln(x, wg, wu, wd, g, b, layer):
    t = x.shape[0]
    row = pl.BlockSpec((ROW_TILE, D_MODEL), lambda i: (i, 0))
    return pl.pallas_call(
        _ffn_ln_kernel,
        grid=(t // ROW_TILE,),
        in_specs=[row, _resident((D_MODEL, D_FF), layer), _resident((D_MODEL, D_FF), layer),
                  _resident((D_FF, D_MODEL), layer), _resident((1, D_MODEL), layer),
                  _resident((1, D_MODEL), layer)],
        out_specs=row,
        out_shape=jax.ShapeDtypeStruct((t, D_MODEL), F32),
        scratch_shapes=[pltpu.VMEM((ROW_TILE, D_MODEL), F32)],
        compiler_params=_params("parallel"),
        name="ffn_ln",
    )(x, wg, wu, wd, g, b)


def _gelu(y):
    return 0.5 * y * (1.0 + lax.erf(y * (2.0 ** -0.5)))


def _inproj_kernel(x_ref, w_ref, cos_ref, sin_ref, lng_ref, lnb_ref, sw_ref, sb_ref,
                   a_ref, ob_ref, c_ref, u_ref):
    lane = lax.broadcasted_iota(jnp.int32, (1, LANES), 1)
    first_half = lane % A_DH < A_DH // 2
    low_lanes = lane < B_DG
    tri_r = lax.broadcasted_iota(jnp.int32, (B_CHUNK, B_CHUNK), 0)
    tri_c = lax.broadcasted_iota(jnp.int32, (B_CHUNK, B_CHUNK), 1)
    w_tril = [jnp.where(tri_r >= tri_c, sw_ref[g], 0.0).astype(BF16) for g in range(B_GROUPS)]
    zero = jnp.zeros((), BF16)
    q_scale = A_DH ** -0.5 * LOG2_E

    xb = x_ref[...].astype(BF16)

    def project(col):
        return _dot(xb, w_ref[:, col:col + MXU_WIDTH])

    def rope_to(out_ref, out_col, scale):
        def epilogue(_, y):
            for hb in range(MXU_WIDTH // LANES):
                yh = y[:, hb * LANES:(hb + 1) * LANES]
                swapped = jnp.where(first_half, pltpu.roll(yh, LANES - A_DH // 2, 1),
                                    pltpu.roll(yh, A_DH // 2, 1))
                roped = (yh * cos_ref[...] + swapped * sin_ref[...]) * scale
                out_ref[:, out_col + hb * LANES:out_col + (hb + 1) * LANES] = roped.astype(BF16)
        return epilogue

    def plain_to(out_ref, out_col):
        def epilogue(_, y):
            out_ref[:, out_col:out_col + MXU_WIDTH] = y.astype(BF16)
        return epilogue

    def gate_input(_, y):
        u_ref[...] = _gelu(y)

    def spatial_gating(_, y):
        vn = _layer_norm(_gelu(y), lng_ref[...], lnb_ref[...]).astype(BF16)
        for ci in range(ROW_TILE // B_CHUNK):
            rsl = slice(ci * B_CHUNK, (ci + 1) * B_CHUNK)
            for hb in range(B_WIDTH // LANES):
                csl = slice(hb * LANES, (hb + 1) * LANES)
                vblk = vn[rsl, csl]
                mix = (_dot(w_tril[2 * hb], jnp.where(low_lanes, vblk, zero))
                       + _dot(w_tril[2 * hb + 1], jnp.where(low_lanes, zero, vblk))
                       + sb_ref[:, csl])
                ob_ref[rsl, csl] = (u_ref[rsl, csl] * mix).astype(BF16)

    c0 = A_COLS + 2 * B_WIDTH
    blocks = [(A_COLS, gate_input), (A_COLS + B_WIDTH, spatial_gating)]
    for j in range(2 * A_HEADS * A_DV // MXU_WIDTH):
        col = j * MXU_WIDTH
        blocks.append((col, rope_to(a_ref, col, q_scale if col < A_HEADS * A_DV else 1.0)))
    for j in range(A_HEADS * A_DV // MXU_WIDTH):
        col = 2 * A_HEADS * A_DV + j * MXU_WIDTH
        blocks.append((col, plain_to(a_ref, col)))
    blocks.append((c0, rope_to(c_ref, 0, C_DH ** -0.5 * LOG2_E)))
    blocks.append((c0 + MXU_WIDTH, rope_to(c_ref, MXU_WIDTH, 1.0)))
    blocks.append((c0 + 2 * MXU_WIDTH, plain_to(c_ref, 2 * MXU_WIDTH)))
    epilogues = dict(blocks)
    _pipelined([col for col, _ in blocks], project, lambda col, y: epilogues[col](col, y), ahead=1)


def _inproj(x, w_in, cos_t, sin_t, ln_g, ln_b, sgu_w, sgu_bias, layer):
    t = x.shape[0]
    rows = lambda n: pl.BlockSpec((ROW_TILE, n), lambda i: (i, 0))
    return pl.pallas_call(
        _inproj_kernel,
        grid=(t // ROW_TILE,),
        in_specs=[rows(D_MODEL), _resident((D_MODEL, IN_COLS), layer), rows(LANES), rows(LANES),
                  _resident((1, B_WIDTH), layer), _resident((1, B_WIDTH), layer),
                  _resident((B_GROUPS, B_CHUNK, B_CHUNK), layer),
                  _resident((B_CHUNK, B_WIDTH), layer)],
        out_specs=[rows(A_COLS), rows(B_WIDTH), rows(C_COLS)],
        out_shape=[jax.ShapeDtypeStruct((t, A_COLS), BF16),
                   jax.ShapeDtypeStruct((t, B_WIDTH), BF16),
                   jax.ShapeDtypeStruct((t, C_COLS), BF16)],
        scratch_shapes=[pltpu.VMEM((ROW_TILE, B_WIDTH), F32)],
        compiler_params=_params("parallel"),
        name="mixer_inproj",
    )(x, w_in, cos_t, sin_t, ln_g, ln_b, sgu_w, sgu_bias)


def _tile_scores(q_m, k_ref, i, tile, causal):
    d0 = i * tile
    s = _dot_nt(q_m, k_ref[0:d0 + tile, :])
    s_d = jnp.where(causal, s[:, d0:], NEG_INF)
    m = jnp.max(s_d, axis=-1, keepdims=True)
    if i == 0:
        return [s_d], m
    s_p = s[:, :d0]
    return [s_p, s_d], jnp.maximum(m, jnp.max(s_p, axis=-1, keepdims=True))


def _tile_output(parts, m, v_ref, i, tile):
    p = [jnp.exp2(s - m) for s in parts]
    denom = jnp.sum(p[0], axis=-1, keepdims=True)
    for extra in p[1:]:
        denom = denom + jnp.sum(extra, axis=-1, keepdims=True)
    p = [x.astype(BF16) for x in p]
    p = p[0] if len(p) == 1 else jnp.concatenate(p, axis=1)
    return _dot(p, v_ref[0:(i + 1) * tile, :]) / denom


PIPELINE_AHEAD = 2


def _pipelined(items, matmul_stage, vector_stage, ahead=PIPELINE_AHEAD):
    outs = {}
    pending = {}
    for j in range(len(items) + ahead):
        if j < len(items):
            pending[items[j]] = matmul_stage(items[j])
        if j >= ahead:
            item = items[j - ahead]
            outs[item] = vector_stage(item, pending.pop(item))
    return outs


def _causal_mask(tile):
    row = lax.broadcasted_iota(jnp.int32, (tile, tile), 0)
    col = lax.broadcasted_iota(jnp.int32, (tile, tile), 1)
    return col <= row


def _low_lanes():
    return lax.broadcasted_iota(jnp.int32, (1, LANES), 1) < LANES // 2


def _map_query(q_ref, i, tile, mp):
    q = q_ref[i * tile:(i + 1) * tile, :]
    keep = _low_lanes() if mp == 0 else jnp.logical_not(_low_lanes())
    return jnp.where(keep, q, jnp.zeros((), q.dtype))


def _diff_attn_kernel(lam_ref, q_ref, k_ref, v_ref, g_ref, o_ref, *, lam_init):
    tile = ATTN_TILE
    n_tiles = q_ref.shape[0] // tile
    causal = _causal_mask(tile)
    lp = lam_ref[...]
    lam = (jnp.exp(jnp.sum(lp[0:1] * lp[1:2], axis=-1, keepdims=True))
           - jnp.exp(jnp.sum(lp[2:3] * lp[3:4], axis=-1, keepdims=True)) + lam_init)
    gain = g_ref[...] * (1.0 - lam_init)
    heads = q_ref.shape[1] // LANES
    head_ref = lambda ref, hd: ref.at[:, hd * LANES:(hd + 1) * LANES]

    def scores(item):
        i, hd, mp = item
        return _tile_scores(_map_query(head_ref(q_ref, hd), i, tile, mp), head_ref(k_ref, hd),
                            i, tile, causal)

    def output(item, scored):
        return _tile_output(*scored, head_ref(v_ref, item[1]), item[0], tile)

    items = [(i, hd, mp) for i in reversed(range(n_tiles)) for hd in range(heads) for mp in range(2)]
    outs = _pipelined(items, scores, output)
    for i in range(n_tiles):
        for hd in range(heads):
            o = outs[(i, hd, 0)] - lam * outs[(i, hd, 1)]
            ms = jnp.mean(o * o, axis=-1, keepdims=True)
            o_ref[i * tile:(i + 1) * tile, hd * LANES:(hd + 1) * LANES] = (
                o * lax.rsqrt(ms + LN_EPS) * gain).astype(BF16)


def _diff_attn(qkv, lam_params, subln_g, lam_init, layer):
    bsz, seq, _ = qkv.shape
    groups = A_HEADS // DIFF_HEADS_PER_STEP
    width = DIFF_HEADS_PER_STEP * LANES
    head = lambda off: pl.BlockSpec((None, seq, width), lambda b, h: (b, 0, off + h))
    return pl.pallas_call(
        functools.partial(_diff_attn_kernel, lam_init=lam_init),
        grid=(bsz, groups),
        in_specs=[_resident((4, A_DH), layer), head(0), head(groups), head(2 * groups),
                  _resident((1, A_DV), layer)],
        out_specs=head(0),
        out_shape=jax.ShapeDtypeStruct((bsz, seq, A_HEADS * A_DV), BF16),
        compiler_params=_params("parallel", "parallel"),
        name="diff_attn",
    )(lam_params, qkv, qkv, qkv, subln_g)


MASKED_SCORE = -1e30


def _moba_penalties(gate, blk):
    n_blocks, seq = gate.shape
    block = lax.broadcasted_iota(jnp.int32, (n_blocks, seq), 0)
    is_past = (block + 1) * blk <= lax.broadcasted_iota(jnp.int32, (n_blocks, seq), 1)
    pen = jnp.zeros((n_blocks, seq), F32)
    for n in range(n_blocks - 1):
        g_n = gate[n:n + 1, :]
        ahead = jnp.where(block < n, jnp.where(gate >= g_n, 1.0, 0.0),
                          jnp.where(gate > g_n, 1.0, 0.0))
        rank = jnp.sum(jnp.where(is_past, ahead, 0.0), axis=0, keepdims=True)
        pen_n = jnp.where(rank < C_TOPK, 0.0, MASKED_SCORE)
        pen = jnp.where((block == n) & is_past, pen_n, pen)
    return pen


def _moba_kernel(q_ref, k_ref, v_ref, o_ref, kaug_ref, qaug_ref):
    blk = C_BLOCK
    seq = q_ref.shape[0]
    n_blocks = seq // blk
    causal = _causal_mask(blk)
    low = _low_lanes()

    q = q_ref[...]
    key = lax.broadcasted_iota(jnp.int32, (n_blocks, seq), 1)
    first_key = lax.broadcasted_iota(jnp.int32, (n_blocks, seq), 0) * blk
    in_block = (key >= first_key) & (key < first_key + blk)
    kbar = _dot(jnp.where(in_block, 1.0 / blk, 0.0).astype(BF16), k_ref[...]).astype(BF16)
    lane = lax.broadcasted_iota(jnp.int32, (1, LANES), 1)
    blk_row = lax.broadcasted_iota(jnp.int32, (n_blocks, LANES), 0)
    blk_lane = lax.broadcasted_iota(jnp.int32, (n_blocks, LANES), 1)
    zero = jnp.zeros((), BF16)
    for hd in range(2):
        own = low if hd == 0 else jnp.logical_not(low)
        lane0 = LANES // 2 if hd == 0 else 0
        for b in range(n_blocks):
            block_id = jnp.where(lane == lane0 + b, 1.0, 0.0).astype(BF16)
            rows = slice(b * blk, (b + 1) * blk)
            kaug_ref[hd, rows, :] = jnp.where(own, k_ref[rows, :], block_id)
        q_m = jnp.where(own, q, zero)
        pen = _moba_penalties(_dot_nt(kbar, q_m), blk).astype(BF16)
        place = jnp.where(blk_lane - lane0 == blk_row, 1.0, 0.0).astype(BF16)
        placed = lax.dot_general(pen, place, (((0,), (0,)), ((), ())),
                                 preferred_element_type=F32)
        qaug_ref[hd] = q_m + placed.astype(BF16)

    def scores(item):
        i, hd = item
        return _tile_scores(qaug_ref[hd, i * blk:(i + 1) * blk, :], kaug_ref.at[hd], i, blk, causal)

    def output(item, scored):
        return _tile_output(*scored, v_ref, item[0], blk)

    items = [(i, hd) for i in reversed(range(n_blocks)) for hd in range(2)]
    outs = _pipelined(items, scores, output)
    for i in range(n_blocks):
        o_ref[i * blk:(i + 1) * blk, :] = jnp.where(low, outs[(i, 0)], outs[(i, 1)]).astype(BF16)


def _moba(qkv):
    bsz, seq, _ = qkv.shape
    pairs = C_HEADS * C_DH // LANES
    head = lambda off: pl.BlockSpec((None, seq, LANES), lambda b, h: (b, 0, off + h))
    return pl.pallas_call(
        _moba_kernel,
        grid=(bsz, pairs),
        in_specs=[head(0), head(pairs), head(2 * pairs)],
        out_specs=head(0),
        out_shape=jax.ShapeDtypeStruct((bsz, seq, C_HEADS * C_DH), BF16),
        scratch_shapes=[pltpu.VMEM((2, seq, LANES), BF16), pltpu.VMEM((2, seq, LANES), BF16)],
        compiler_params=_params("parallel", "parallel"),
        name="moba_attn",
    )(qkv, qkv, qkv)


def _mem_kv_kernel(mem_ref, wk_ref, wv_ref, k_ref, v_ref):
    mb = mem_ref[...].astype(BF16)
    k_ref[...] = _dot(mb, wk_ref[...]).astype(BF16)
    v_ref[...] = _dot(mb, wv_ref[...]).astype(BF16)


def _mem_kv(mem2d, wk, wv):
    rows = mem2d.shape[0]
    w = pl.BlockSpec((None, D_MODEL, D_MODEL), lambda l, i: (l, 0, 0))
    out = pl.BlockSpec((None, ROW_TILE, D_MODEL), lambda l, i: (l, i, 0))
    shape = jax.ShapeDtypeStruct((DEPTH, rows, D_MODEL), BF16)
    return pl.pallas_call(
        _mem_kv_kernel,
        grid=(DEPTH, rows // ROW_TILE),
        in_specs=[pl.BlockSpec((ROW_TILE, D_MODEL), lambda l, i: (i, 0)), w, w],
        out_specs=[out, out],
        out_shape=[shape, shape],
        compiler_params=_params("parallel", "parallel"),
        name="mem_kv_proj",
    )(mem2d, wk, wv)


XATTN_PARTS = D_MODEL // MXU_WIDTH


def _mixout_xattn_kernel(x_ref, oa_ref, ob_ref, oc_ref, wout_ref, g2_ref, b2_ref,
                         k_ref, v_ref, wq_ref, wo_ref, g_ref, b_ref, o_ref,
                         res_ref, q_ref, cat_ref, y_ref):
    na = A_HEADS * A_DV
    col_blocks = [slice(c * MXU_WIDTH, (c + 1) * MXU_WIDTH) for c in range(XATTN_PARTS)]

    def sub_tile(rows):
        chunk = (rows.stop - rows.start) // XATTN_PARTS
        row_chunks = [slice(rows.start + j * chunk, rows.start + (j + 1) * chunk)
                      for j in range(XATTN_PARTS)]
        for cols in col_blocks:
            mix = (_dot(oa_ref[rows, :], wout_ref[:na, cols])
                   + _dot(ob_ref[rows, :], wout_ref[na:na + B_WIDTH, cols])
                   + _dot(oc_ref[rows, :], wout_ref[na + B_WIDTH:, cols]))
            res_ref[rows, cols] = DEEPNORM_ALPHA * x_ref[rows, cols] + mix
            yield
        for rr in row_chunks:
            res_ref[rr, :] = _layer_norm(res_ref[rr, :], g2_ref[...], b2_ref[...])
            yield
        xb = res_ref[rows, :].astype(BF16)
        for cols in col_blocks:
            q_ref[rows, cols] = (_dot(xb, wq_ref[:, cols]) * (XA_DH ** -0.5 * LOG2_E)).astype(BF16)
            yield
        for h in range(XA_HEADS):
            sl = slice(h * XA_DH, (h + 1) * XA_DH)
            s = _dot_nt(q_ref[rows, sl], k_ref[:, sl])
            p = jnp.exp2(s - jnp.max(s, axis=-1, keepdims=True))
            denom = jnp.sum(p, axis=-1, keepdims=True)
            cat_ref[rows, sl] = (_dot(p.astype(BF16), v_ref[:, sl]) / denom).astype(BF16)
            yield
        attended = cat_ref[rows, :]
        for cols in col_blocks:
            y_ref[rows, cols] = _dot(attended, wo_ref[:, cols])
            yield
        for j, rr in enumerate(row_chunks):
            o_ref[rr, :] = _layer_norm(DEEPNORM_ALPHA * res_ref[rr, :] + y_ref[rr, :],
                                       g_ref[...], b_ref[...])
            if j + 1 < XATTN_PARTS:
                yield

    _interleave_sub_tiles(sub_tile, skew=XATTN_PARTS)


def _mixout_xattn(x3, oa, ob, oc, w_out, g2, b2, k, v, wq, wo, g3, b3, layer):
    bsz, seq, _ = x3.shape
    rows = lambda n: pl.BlockSpec((None, ROW_TILE, n), lambda b_, i: (b_, i, 0))
    kv = pl.BlockSpec((None, None, MEM_LEN, D_MODEL), lambda b_, i: (layer, b_, 0, 0))
    weight = _resident((D_MODEL, D_MODEL), layer)
    vec = _resident((1, D_MODEL), layer)
    return pl.pallas_call(
        _mixout_xattn_kernel,
        grid=(bsz, seq // ROW_TILE),
        in_specs=[rows(D_MODEL), rows(A_HEADS * A_DV), rows(B_WIDTH), rows(C_HEADS * C_DH),
                  weight, vec, vec, kv, kv, weight, weight, vec, vec],
        out_specs=rows(D_MODEL),
        out_shape=jax.ShapeDtypeStruct((bsz, seq, D_MODEL), F32),
        scratch_shapes=[pltpu.VMEM((ROW_TILE, D_MODEL), F32), pltpu.VMEM((ROW_TILE, D_MODEL), BF16),
                        pltpu.VMEM((ROW_TILE, D_MODEL), BF16), pltpu.VMEM((ROW_TILE, D_MODEL), F32)],
        compiler_params=_params("parallel", "parallel"),
        name="mixout_xattn_ln",
    )(x3, oa, ob, oc, w_out, g2, b2, k, v, wq, wo, g3, b3)


def kernel(x, mem, positions, ffn1_w_gate, ffn1_w_up, ffn1_w_down, ln1_g, ln1_b, mix_w_in, diff_lq1, diff_lk1, diff_lq2, diff_lk2, diff_subln_g, sgu_ln_g, sgu_ln_b, sgu_w, sgu_b, mix_w_out, ln2_g, ln2_b, xa_wq, xa_wk, xa_wv, xa_wo, ln3_g, ln3_b, ffn2_w_gate, ffn2_w_up, ffn2_w_down, ln4_g, ln4_b):
    bsz, seq, d = x.shape
    t = bsz * seq
    bf = lambda w: w.astype(BF16)
    vec = lambda p: p.reshape(DEPTH, 1, -1)

    cos_t, sin_t = _rope_tables(positions)
    mem_k, mem_v = _mem_kv(mem.reshape(bsz * MEM_LEN, d), bf(xa_wk), bf(xa_wv))
    mem_k = mem_k.reshape(DEPTH, bsz, MEM_LEN, d)
    mem_v = mem_v.reshape(DEPTH, bsz, MEM_LEN, d)
    sgu_bias = jnp.repeat(jnp.swapaxes(sgu_b, 1, 2), B_DG, axis=2)
    lam_params = jnp.stack([diff_lq1, diff_lk1, diff_lq2, diff_lk2], axis=1)
    ffn1 = (bf(ffn1_w_gate), bf(ffn1_w_up), bf(ffn1_w_down), vec(ln1_g), vec(ln1_b))
    ffn2 = (bf(ffn2_w_gate), bf(ffn2_w_up), bf(ffn2_w_down), vec(ln4_g), vec(ln4_b))
    w_in, w_out, wq, wo = bf(mix_w_in), bf(mix_w_out), bf(xa_wq), bf(xa_wo)

    h = x.reshape(t, d)
    for l in range(DEPTH):
        lam_init = 0.8 - 0.6 * math.exp(-0.3 * l)
        h = _ffn_ln(h, *ffn1, l)
        qkv_a, ob, qkv_c = _inproj(h, w_in, cos_t, sin_t, vec(sgu_ln_g), vec(sgu_ln_b),
                                   sgu_w, sgu_bias, l)
        oa = _diff_attn(qkv_a.reshape(bsz, seq, A_COLS), lam_params, vec(diff_subln_g), lam_init, l)
        oc = _moba(qkv_c.reshape(bsz, seq, C_COLS))
        h = _mixout_xattn(h.reshape(bsz, seq, d), oa, ob.reshape(bsz, seq, B_WIDTH), oc,
                          w_out, vec(ln2_g), vec(ln2_b), mem_k, mem_v, wq, wo,
                          vec(ln3_g), vec(ln3_b), l).reshape(t, d)
        h = _ffn_ln(h, *ffn2, l)
    return h.reshape(bsz, seq, d)
```

```python
import functools
import math

import jax
import jax.numpy as jnp
from jax import lax
from jax.experimental import pallas as pl
from jax.experimental.pallas import tpu as pltpu

D_MODEL = 1024
DEPTH = 2
MEM_LEN = 256
A_HEADS = 4
A_DH = 64
A_DV = 2 * A_DH
B_GROUPS = 4
B_DG = 64
B_WIDTH = B_GROUPS * B_DG
B_CHUNK = 128
C_HEADS = 4
C_DH = 64
C_BLOCK = 256
C_TOPK = 3
A_COLS = 3 * A_HEADS * A_DV
C_COLS = 3 * C_HEADS * C_DH
IN_COLS = A_COLS + 2 * B_WIDTH + C_COLS
XA_HEADS = 4
XA_DH = D_MODEL // XA_HEADS
D_FF = 2816
ROPE_THETA = 10000.0
LN_EPS = 1e-5
DEEPNORM_ALPHA = (2.0 * DEPTH) ** 0.25

LANES = 128
MXU_WIDTH = 256
VMEM_LIMIT_BYTES = 56 * 1024 * 1024

ROW_TILE = 1024
SUB_TILE = 512
FF_CHUNK = MXU_WIDTH
ATTN_TILE = 256

F32 = jnp.float32
BF16 = jnp.bfloat16
NEG_INF = float("-inf")
LOG2_E = math.log2(math.e)


def _params(*sem):
    return pltpu.CompilerParams(dimension_semantics=sem, vmem_limit_bytes=VMEM_LIMIT_BYTES)


def _resident(shape, layer):
    index = (layer,) + (0,) * len(shape)
    return pl.BlockSpec((None,) + tuple(shape), lambda *_: index, pipeline_mode=pl.Buffered(1))


def _layer_norm(y, g, b, eps=LN_EPS):
    mu = jnp.mean(y, axis=-1, keepdims=True)
    d = y - mu
    var = jnp.mean(d * d, axis=-1, keepdims=True)
    return d * lax.rsqrt(var + eps) * g + b


def _interleave_sub_tiles(stages, sub_tile=None, skew=1):
    sub_tile = sub_tile or SUB_TILE
    gens = [stages(slice(s * sub_tile, (s + 1) * sub_tile)) for s in range(ROW_TILE // sub_tile)]
    live = list(range(len(gens)))
    step = 0
    while live:
        for s in list(live):
            if step >= s * skew:
                try:
                    next(gens[s])
                except StopIteration:
                    live.remove(s)
        step += 1


def _dot(a, b):
    return jnp.dot(a, b, preferred_element_type=F32)


def _dot_nt(a, b):
    return lax.dot_general(a, b, (((1,), (1,)), ((), ())), preferred_element_type=F32)


ROPE_PACK = LANES // (A_DH // 2)


def _rope_kernel(pos_ref, invf_ref, cos_ref, sin_ref):
    half = A_DH // 2
    ang = pos_ref[...] * invf_ref[...]
    lane = lax.broadcasted_iota(jnp.int32, (1, LANES), 1)
    group = lane // half
    sign = jnp.where(lane % A_DH < half, -1.0, 1.0).astype(F32)
    packed_rows = pos_ref.shape[0]
    for table, out_ref, scale in ((jnp.cos(ang), cos_ref, None), (jnp.sin(ang), sin_ref, sign)):
        rolled = [table] + [pltpu.roll(table, half * k, 1) for k in range(1, ROPE_PACK)]
        for j in range(ROPE_PACK):
            spread = rolled[(0 - j) % ROPE_PACK]
            for g in range(1, ROPE_PACK):
                spread = jnp.where(group == g, rolled[(g - j) % ROPE_PACK], spread)
            if scale is not None:
                spread = spread * scale
            out_ref[pl.ds(j, packed_rows, stride=ROPE_PACK), :] = spread


def _rope_tables(positions):
    t = positions.size
    half = A_DH // 2
    inv_freq = 1.0 / (ROPE_THETA ** (jnp.arange(0, A_DH, 2, dtype=F32) / A_DH))
    invf = jnp.tile(inv_freq, ROPE_PACK).reshape(1, LANES)
    pos = jnp.repeat(positions.astype(F32).reshape(t // ROPE_PACK, ROPE_PACK), half, axis=1)
    tm = 2048
    out = jax.ShapeDtypeStruct((t, LANES), F32)
    return pl.pallas_call(
        _rope_kernel,
        grid=(t // tm,),
        in_specs=[pl.BlockSpec((tm // ROPE_PACK, LANES), lambda i: (i, 0)),
                  pl.BlockSpec((1, LANES), lambda i: (0, 0))],
        out_specs=[pl.BlockSpec((tm, LANES), lambda i: (i, 0))] * 2,
        out_shape=[out, out],
        compiler_params=_params("parallel"),
        name="rope_tables",
    )(pos, invf)


def _ffn_ln_kernel(x_ref, wg_ref, wu_ref, wd_ref, g_ref, b_ref, o_ref, acc_ref):
    def sub_tile(rows):
        xb = x_ref[rows, :].astype(BF16)
        for c in range(D_FF // FF_CHUNK):
            cols = slice(c * FF_CHUNK, (c + 1) * FF_CHUNK)
            gate = _dot(xb, wg_ref[:, cols])
            up = _dot(xb, wu_ref[:, cols])
            act = (gate * jax.nn.sigmoid(gate) * up).astype(BF16)
            part = _dot(act, wd_ref[cols, :])
            if c == 0:
                acc_ref[rows, :] = (2.0 * DEEPNORM_ALPHA) * x_ref[rows, :] + part
            else:
                acc_ref[rows, :] += part
            yield
        o_ref[rows, :] = _layer_norm(acc_ref[rows, :], g_ref[...], b_ref[...], eps=4.0 * LN_EPS)

    _interleave_sub_tiles(sub_tile)


def _ffn_ln(x, wg, wu, wd, g, b, layer):
    t = x.shape[0]
    row = pl.BlockSpec((ROW_TILE, D_MODEL), lambda i: (i, 0))
    return pl.pallas_call(
        _ffn_ln_kernel,
        grid=(t // ROW_TILE,),
        in_specs=[row, _resident((D_MODEL, D_FF), layer), _resident((D_MODEL, D_FF), layer),
                  _resident((D_FF, D_MODEL), layer), _resident((1, D_MODEL), layer),
                  _resident((1, D_MODEL), layer)],
        out_specs=row,
        out_shape=jax.ShapeDtypeStruct((t, D_MODEL), F32),
        scratch_shapes=[pltpu.VMEM((ROW_TILE, D_MODEL), F32)],
        compiler_params=_params("parallel"),
        name="ffn_ln",
    )(x, wg, wu, wd, g, b)


def _gelu(y):
    return 0.5 * y * (1.0 + lax.erf(y * (2.0 ** -0.5)))


def _inproj_kernel(x_ref, w_ref, cos_ref, sin_ref, lng_ref, lnb_ref, sw_ref, sb_ref,
                   a_ref, ob_ref, c_ref, u_ref):
    lane = lax.broadcasted_iota(jnp.int32, (1, LANES), 1)
    first_half = lane % A_DH < A_DH // 2
    low_lanes = lane < B_DG
    tri_r = lax.broadcasted_iota(jnp.int32, (B_CHUNK, B_CHUNK), 0)
    tri_c = lax.broadcasted_iota(jnp.int32, (B_CHUNK, B_CHUNK), 1)
    w_tril = [jnp.where(tri_r >= tri_c, sw_ref[g], 0.0).astype(BF16) for g in range(B_GROUPS)]
    zero = jnp.zeros((), BF16)
    q_scale = A_DH ** -0.5 * LOG2_E

    xb = x_ref[...].astype(BF16)

    def project(col):
        return _dot(xb, w_ref[:, col:col + MXU_WIDTH])

    def rope_to(out_ref, out_col, scale):
        def epilogue(_, y):
            for hb in range(MXU_WIDTH // LANES):
                yh = y[:, hb * LANES:(hb + 1) * LANES]
                swapped = jnp.where(first_half, pltpu.roll(yh, LANES - A_DH // 2, 1),
                                    pltpu.roll(yh, A_DH // 2, 1))
                roped = (yh * cos_ref[...] + swapped * sin_ref[...]) * scale
                out_ref[:, out_col + hb * LANES:out_col + (hb + 1) * LANES] = roped.astype(BF16)
        return epilogue

    def plain_to(out_ref, out_col):
        def epilogue(_, y):
            out_ref[:, out_col:out_col + MXU_WIDTH] = y.astype(BF16)
        return epilogue

    def gate_input(_, y):
        u_ref[...] = _gelu(y)

    def spatial_gating(_, y):
        vn = _layer_norm(_gelu(y), lng_ref[...], lnb_ref[...]).astype(BF16)
        for ci in range(ROW_TILE // B_CHUNK):
            rsl = slice(ci * B_CHUNK, (ci + 1) * B_CHUNK)
            for hb in range(B_WIDTH // LANES):
                csl = slice(hb * LANES, (hb + 1) * LANES)
                vblk = vn[rsl, csl]
                mix = (_dot(w_tril[2 * hb], jnp.where(low_lanes, vblk, zero))
                       + _dot(w_tril[2 * hb + 1], jnp.where(low_lanes, zero, vblk))
                       + sb_ref[:, csl])
                ob_ref[rsl, csl] = (u_ref[rsl, csl] * mix).astype(BF16)

    c0 = A_COLS + 2 * B_WIDTH
    blocks = [(A_COLS, gate_input), (A_COLS + B_WIDTH, spatial_gating)]
    for j in range(2 * A_HEADS * A_DV // MXU_WIDTH):
        col = j * MXU_WIDTH
        blocks.append((col, rope_to(a_ref, col, q_scale if col < A_HEADS * A_DV else 1.0)))
    for j in range(A_HEADS * A_DV // MXU_WIDTH):
        col = 2 * A_HEADS * A_DV + j * MXU_WIDTH
        blocks.append((col, plain_to(a_ref, col)))
    blocks.append((c0, rope_to(c_ref, 0, C_DH ** -0.5 * LOG2_E)))
    blocks.append((c0 + MXU_WIDTH, rope_to(c_ref, MXU_WIDTH, 1.0)))
    blocks.append((c0 + 2 * MXU_WIDTH, plain_to(c_ref, 2 * MXU_WIDTH)))
    epilogues = dict(blocks)
    _pipelined([col for col, _ in blocks], project, lambda col, y: epilogues[col](col, y), ahead=1)


def _inproj(x, w_in, cos_t, sin_t, ln_g, ln_b, sgu_w, sgu_bias, layer):
    t = x.shape[0]
    rows = lambda n: pl.BlockSpec((ROW_TILE, n), lambda i: (i, 0))
    return pl.pallas_call(
        _inproj_kernel,
        grid=(t // ROW_TILE,),
        in_specs=[rows(D_MODEL), _resident((D_MODEL, IN_COLS), layer), rows(LANES), rows(LANES),
                  _resident((1, B_WIDTH), layer), _resident((1, B_WIDTH), layer),
                  _resident((B_GROUPS, B_CHUNK, B_CHUNK), layer),
                  _resident((B_CHUNK, B_WIDTH), layer)],
        out_specs=[rows(A_COLS), rows(B_WIDTH), rows(C_COLS)],
        out_shape=[jax.ShapeDtypeStruct((t, A_COLS), BF16),
                   jax.ShapeDtypeStruct((t, B_WIDTH), BF16),
                   jax.ShapeDtypeStruct((t, C_COLS), BF16)],
        scratch_shapes=[pltpu.VMEM((ROW_TILE, B_WIDTH), F32)],
        compiler_params=_params("parallel"),
        name="mixer_inproj",
    )(x, w_in, cos_t, sin_t, ln_g, ln_b, sgu_w, sgu_bias)


def _tile_scores(q_m, k_ref, i, tile, causal):
    d0 = i * tile
    s = _dot_nt(q_m, k_ref[0:d0 + tile, :])
    s_d = jnp.where(causal, s[:, d0:], NEG_INF)
    m = jnp.max(s_d, axis=-1, keepdims=True)
    if i == 0:
        return [s_d], m
    s_p = s[:, :d0]
    return [s_p, s_d], jnp.maximum(m, jnp.max(s_p, axis=-1, keepdims=True))


def _tile_output(parts, m, v_ref, i, tile):
    p = [jnp.exp2(s - m) for s in parts]
    denom = jnp.sum(p[0], axis=-1, keepdims=True)
    for extra in p[1:]:
        denom = denom + jnp.sum(extra, axis=-1, keepdims=True)
    p = [x.astype(BF16) for x in p]
    p = p[0] if len(p) == 1 else jnp.concatenate(p, axis=1)
    return _dot(p, v_ref[0:(i + 1) * tile, :]) / denom


PIPELINE_AHEAD = 2


def _pipelined(items, matmul_stage, vector_stage, ahead=PIPELINE_AHEAD):
    outs = {}
    pending = {}
    for j in range(len(items) + ahead):
        if j < len(items):
            pending[items[j]] = matmul_stage(items[j])
        if j >= ahead:
            item = items[j - ahead]
            outs[item] = vector_stage(item, pending.pop(item))
    return outs


def _causal_mask(tile):
    row = lax.broadcasted_iota(jnp.int32, (tile, tile), 0)
    col = lax.broadcasted_iota(jnp.int32, (tile, tile), 1)
    return col <= row


def _low_lanes():
    return lax.broadcasted_iota(jnp.int32, (1, LANES), 1) < LANES // 2


def _map_query(q_ref, i, tile, mp):
    q = q_ref[i * tile:(i + 1) * tile, :]
    keep = _low_lanes() if mp == 0 else jnp.logical_not(_low_lanes())
    return jnp.where(keep, q, jnp.zeros((), q.dtype))


def _diff_attn_kernel(lam_ref, q_ref, k_ref, v_ref, g_ref, o_ref, *, lam_init):
    tile = ATTN_TILE
    n_tiles = q_ref.shape[0] // tile
    causal = _causal_mask(tile)
    lp = lam_ref[...]
    lam = (jnp.exp(jnp.sum(lp[0:1] * lp[1:2], axis=-1, keepdims=True))
           - jnp.exp(jnp.sum(lp[2:3] * lp[3:4], axis=-1, keepdims=True)) + lam_init)
    gain = g_ref[...] * (1.0 - lam_init)

    def scores(item):
        i, mp = item
        return _tile_scores(_map_query(q_ref, i, tile, mp), k_ref, i, tile, causal)

    def output(item, scored):
        return _tile_output(*scored, v_ref, item[0], tile)

    items = [(i, mp) for i in reversed(range(n_tiles)) for mp in range(2)]
    outs = _pipelined(items, scores, output)
    for i in range(n_tiles):
        o = outs[(i, 0)] - lam * outs[(i, 1)]
        ms = jnp.mean(o * o, axis=-1, keepdims=True)
        o_ref[i * tile:(i + 1) * tile, :] = (o * lax.rsqrt(ms + LN_EPS) * gain).astype(BF16)


def _diff_attn(qkv, lam_params, subln_g, lam_init, layer):
    bsz, seq, _ = qkv.shape
    head = lambda off: pl.BlockSpec((None, seq, LANES), lambda b, h: (b, 0, off + h))
    return pl.pallas_call(
        functools.partial(_diff_attn_kernel, lam_init=lam_init),
        grid=(bsz, A_HEADS),
        in_specs=[_resident((4, A_DH), layer), head(0), head(A_HEADS), head(2 * A_HEADS),
                  _resident((1, A_DV), layer)],
        out_specs=head(0),
        out_shape=jax.ShapeDtypeStruct((bsz, seq, A_HEADS * A_DV), BF16),
        compiler_params=_params("parallel", "parallel"),
        name="diff_attn",
    )(lam_params, qkv, qkv, qkv, subln_g)


MASKED_SCORE = -1e30


def _moba_penalties(gate, blk):
    n_blocks, seq = gate.shape
    block = lax.broadcasted_iota(jnp.int32, (n_blocks, seq), 0)
    is_past = (block + 1) * blk <= lax.broadcasted_iota(jnp.int32, (n_blocks, seq), 1)
    pen = jnp.zeros((n_blocks, seq), F32)
    for n in range(n_blocks - 1):
        g_n = gate[n:n + 1, :]
        ahead = jnp.where(block < n, jnp.where(gate >= g_n, 1.0, 0.0),
                          jnp.where(gate > g_n, 1.0, 0.0))
        rank = jnp.sum(jnp.where(is_past, ahead, 0.0), axis=0, keepdims=True)
        pen_n = jnp.where(rank < C_TOPK, 0.0, MASKED_SCORE)
        pen = jnp.where((block == n) & is_past, pen_n, pen)
    return pen


def _moba_kernel(q_ref, k_ref, v_ref, o_ref, kaug_ref, qaug_ref):
    blk = C_BLOCK
    seq = q_ref.shape[0]
    n_blocks = seq // blk
    causal = _causal_mask(blk)
    low = _low_lanes()

    q = q_ref[...]
    key = lax.broadcasted_iota(jnp.int32, (n_blocks, seq), 1)
    first_key = lax.broadcasted_iota(jnp.int32, (n_blocks, seq), 0) * blk
    in_block = (key >= first_key) & (key < first_key + blk)
    kbar = _dot(jnp.where(in_block, 1.0 / blk, 0.0).astype(BF16), k_ref[...]).astype(BF16)
    lane = lax.broadcasted_iota(jnp.int32, (1, LANES), 1)
    blk_row = lax.broadcasted_iota(jnp.int32, (n_blocks, LANES), 0)
    blk_lane = lax.broadcasted_iota(jnp.int32, (n_blocks, LANES), 1)
    zero = jnp.zeros((), BF16)
    for hd in range(2):
        own = low if hd == 0 else jnp.logical_not(low)
        lane0 = LANES // 2 if hd == 0 else 0
        for b in range(n_blocks):
            block_id = jnp.where(lane == lane0 + b, 1.0, 0.0).astype(BF16)
            rows = slice(b * blk, (b + 1) * blk)
            kaug_ref[hd, rows, :] = jnp.where(own, k_ref[rows, :], block_id)
        q_m = jnp.where(own, q, zero)
        pen = _moba_penalties(_dot_nt(kbar, q_m), blk).astype(BF16)
        place = jnp.where(blk_lane - lane0 == blk_row, 1.0, 0.0).astype(BF16)
        placed = lax.dot_general(pen, place, (((0,), (0,)), ((), ())),
                                 preferred_element_type=F32)
        qaug_ref[hd] = q_m + placed.astype(BF16)

    def scores(item):
        i, hd = item
        return _tile_scores(qaug_ref[hd, i * blk:(i + 1) * blk, :], kaug_ref.at[hd], i, blk, causal)

    def output(item, scored):
        return _tile_output(*scored, v_ref, item[0], blk)

    items = [(i, hd) for i in reversed(range(n_blocks)) for hd in range(2)]
    outs = _pipelined(items, scores, output)
    for i in range(n_blocks):
        o_ref[i * blk:(i + 1) * blk, :] = jnp.where(low, outs[(i, 0)], outs[(i, 1)]).astype(BF16)


def _moba(qkv):
    bsz, seq, _ = qkv.shape
    pairs = C_HEADS * C_DH // LANES
    head = lambda off: pl.BlockSpec((None, seq, LANES), lambda b, h: (b, 0, off + h))
    return pl.pallas_call(
        _moba_kernel,
        grid=(bsz, pairs),
        in_specs=[head(0), head(pairs), head(2 * pairs)],
        out_specs=head(0),
        out_shape=jax.ShapeDtypeStruct((bsz, seq, C_HEADS * C_DH), BF16),
        scratch_shapes=[pltpu.VMEM((2, seq, LANES), BF16), pltpu.VMEM((2, seq, LANES), BF16)],
        compiler_params=_params("parallel", "parallel"),
        name="moba_attn",
    )(qkv, qkv, qkv)


def _mem_kv_kernel(mem_ref, wk_ref, wv_ref, k_ref, v_ref):
    mb = mem_ref[...].astype(BF16)
    k_ref[...] = _dot(mb, wk_ref[...].astype(BF16)).astype(BF16)
    v_ref[...] = _dot(mb, wv_ref[...].astype(BF16)).astype(BF16)


def _mem_kv(mem2d, wk, wv):
    rows = mem2d.shape[0]
    w = pl.BlockSpec((None, D_MODEL, D_MODEL), lambda l, i: (l, 0, 0))
    out = pl.BlockSpec((None, ROW_TILE, D_MODEL), lambda l, i: (l, i, 0))
    shape = jax.ShapeDtypeStruct((DEPTH, rows, D_MODEL), BF16)
    return pl.pallas_call(
        _mem_kv_kernel,
        grid=(DEPTH, rows // ROW_TILE),
        in_specs=[pl.BlockSpec((ROW_TILE, D_MODEL), lambda l, i: (i, 0)), w, w],
        out_specs=[out, out],
        out_shape=[shape, shape],
        compiler_params=_params("parallel", "parallel"),
        name="mem_kv_proj",
    )(mem2d, wk, wv)


XATTN_PARTS = D_MODEL // MXU_WIDTH


def _mixout_xattn_kernel(x_ref, oa_ref, ob_ref, oc_ref, wout_ref, g2_ref, b2_ref,
                         k_ref, v_ref, wq_ref, wo_ref, g_ref, b_ref, o_ref,
                         res_ref, q_ref, cat_ref, y_ref):
    na = A_HEADS * A_DV
    col_blocks = [slice(c * MXU_WIDTH, (c + 1) * MXU_WIDTH) for c in range(XATTN_PARTS)]

    def sub_tile(rows):
        chunk = (rows.stop - rows.start) // XATTN_PARTS
        row_chunks = [slice(rows.start + j * chunk, rows.start + (j + 1) * chunk)
                      for j in range(XATTN_PARTS)]
        for cols in col_blocks:
            mix = (_dot(oa_ref[rows, :], wout_ref[:na, cols])
                   + _dot(ob_ref[rows, :], wout_ref[na:na + B_WIDTH, cols])
                   + _dot(oc_ref[rows, :], wout_ref[na + B_WIDTH:, cols]))
            res_ref[rows, cols] = DEEPNORM_ALPHA * x_ref[rows, cols] + mix
            yield
        for rr in row_chunks:
            res_ref[rr, :] = _layer_norm(res_ref[rr, :], g2_ref[...], b2_ref[...])
            yield
        xb = res_ref[rows, :].astype(BF16)
        for cols in col_blocks:
            q_ref[rows, cols] = (_dot(xb, wq_ref[:, cols]) * (XA_DH ** -0.5 * LOG2_E)).astype(BF16)
            yield
        for h in range(XA_HEADS):
            sl = slice(h * XA_DH, (h + 1) * XA_DH)
            s = _dot_nt(q_ref[rows, sl], k_ref[:, sl])
            p = jnp.exp2(s - jnp.max(s, axis=-1, keepdims=True))
            denom = jnp.sum(p, axis=-1, keepdims=True)
            cat_ref[rows, sl] = (_dot(p.astype(BF16), v_ref[:, sl]) / denom).astype(BF16)
            yield
        attended = cat_ref[rows, :]
        for cols in col_blocks:
            y_ref[rows, cols] = _dot(attended, wo_ref[:, cols])
            yield
        for j, rr in enumerate(row_chunks):
            o_ref[rr, :] = _layer_norm(DEEPNORM_ALPHA * res_ref[rr, :] + y_ref[rr, :],
                                       g_ref[...], b_ref[...])
            if j + 1 < XATTN_PARTS:
                yield

    _interleave_sub_tiles(sub_tile, skew=XATTN_PARTS)


def _mixout_xattn(x3, oa, ob, oc, w_out, g2, b2, k, v, wq, wo, g3, b3, layer):
    bsz, seq, _ = x3.shape
    rows = lambda n: pl.BlockSpec((None, ROW_TILE, n), lambda b_, i: (b_, i, 0))
    kv = pl.BlockSpec((None, None, MEM_LEN, D_MODEL), lambda b_, i: (layer, b_, 0, 0))
    weight = _resident((D_MODEL, D_MODEL), layer)
    vec = _resident((1, D_MODEL), layer)
    return pl.pallas_call(
        _mixout_xattn_kernel,
        grid=(bsz, seq // ROW_TILE),
        in_specs=[rows(D_MODEL), rows(A_HEADS * A_DV), rows(B_WIDTH), rows(C_HEADS * C_DH),
                  weight, vec, vec, kv, kv, weight, weight, vec, vec],
        out_specs=rows(D_MODEL),
        out_shape=jax.ShapeDtypeStruct((bsz, seq, D_MODEL), F32),
        scratch_shapes=[pltpu.VMEM((ROW_TILE, D_MODEL), F32), pltpu.VMEM((ROW_TILE, D_MODEL), BF16),
                        pltpu.VMEM((ROW_TILE, D_MODEL), BF16), pltpu.VMEM((ROW_TILE, D_MODEL), F32)],
        compiler_params=_params("parallel", "parallel"),
        name="mixout_xattn_ln",
    )(x3, oa, ob, oc, w_out, g2, b2, k, v, wq, wo, g3, b3)


def kernel(x, mem, positions, ffn1_w_gate, ffn1_w_up, ffn1_w_down, ln1_g, ln1_b, mix_w_in, diff_lq1, diff_lk1, diff_lq2, diff_lk2, diff_subln_g, sgu_ln_g, sgu_ln_b, sgu_w, sgu_b, mix_w_out, ln2_g, ln2_b, xa_wq, xa_wk, xa_wv, xa_wo, ln3_g, ln3_b, ffn2_w_gate, ffn2_w_up, ffn2_w_down, ln4_g, ln4_b):
    bsz, seq, d = x.shape
    t = bsz * seq
    bf = lambda w: w.astype(BF16)
    vec = lambda p: p.reshape(DEPTH, 1, -1)

    cos_t, sin_t = _rope_tables(positions)
    mem_k, mem_v = _mem_kv(mem.reshape(bsz * MEM_LEN, d), xa_wk, xa_wv)
    mem_k = mem_k.reshape(DEPTH, bsz, MEM_LEN, d)
    mem_v = mem_v.reshape(DEPTH, bsz, MEM_LEN, d)
    sgu_bias = jnp.repeat(jnp.swapaxes(sgu_b, 1, 2), B_DG, axis=2)
    lam_params = jnp.stack([diff_lq1, diff_lk1, diff_lq2, diff_lk2], axis=1)
    ffn1 = (bf(ffn1_w_gate), bf(ffn1_w_up), bf(ffn1_w_down), vec(ln1_g), vec(ln1_b))
    ffn2 = (bf(ffn2_w_gate), bf(ffn2_w_up), bf(ffn2_w_down), vec(ln4_g), vec(ln4_b))
    w_in, w_out, wq, wo = bf(mix_w_in), bf(mix_w_out), bf(xa_wq), bf(xa_wo)

    h = x.reshape(t, d)
    for l in range(DEPTH):
        lam_init = 0.8 - 0.6 * math.exp(-0.3 * l)
        h = _ffn_ln(h, *ffn1, l)
        qkv_a, ob, qkv_c = _inproj(h, w_in, cos_t, sin_t, vec(sgu_ln_g), vec(sgu_ln_b),
                                   sgu_w, sgu_bias, l)
        oa = _diff_attn(qkv_a.reshape(bsz, seq, A_COLS), lam_params, vec(diff_subln_g), lam_init, l)
        oc = _moba(qkv_c.reshape(bsz, seq, C_COLS))
        h = _mixout_xattn(h.reshape(bsz, seq, d), oa, ob.reshape(bsz, seq, B_WIDTH), oc,
                          w_out, vec(ln2_g), vec(ln2_b), mem_k, mem_v, wq, wo,
                          vec(ln3_g), vec(ln3_b), l).reshape(t, d)
        h = _ffn_ln(h, *ffn2, l)
    return h.reshape(bsz, seq, d)
```
